```python
import math
import jax, jax.numpy as jnp
from jax import lax
import numpy as np

D_MODEL = 2048
BATCH = 4
SEQ = 2048
DEPTH = 2

CHUNK = 64
N_MIXERS = 2
N_FOX = (DEPTH + 1) // 2
N_RWKV = DEPTH // 2

FOX_HEADS = 16
FOX_HEAD_DIM = D_MODEL // FOX_HEADS
Q_BLOCK = 128

RWKV_HEAD_DIM = 64
RWKV_HEADS = D_MODEL // RWKV_HEAD_DIM
DECAY_LORA = 96
AAA_LORA = 96
GATE_LORA = 256

D_FF = 5632
CONV_WIDTH = 3
NORM_EPS = 1e-6
GN_EPS = 64e-5

kernel_name = "hybrid_fox_rwkv7_convffn_adaln"


def rms_norm(x, g):
    xf = x.astype(jnp.float32)
    y = xf * lax.rsqrt(jnp.mean(xf * xf, axis=-1, keepdims=True) + NORM_EPS)
    return (y * g.astype(jnp.float32)).astype(x.dtype)


def token_shift(x):
    return jnp.pad(x, ((0, 0), (1, 0), (0, 0)))[:, :-1]


def fox_attention(h, w_in, b_f, w_out):
    B, S, D = h.shape
    H, hd = FOX_HEADS, FOX_HEAD_DIM
    proj = h @ w_in
    q, k, v, f_logit = jnp.split(proj, [D, 2 * D, 3 * D], axis=-1)
    q = q.reshape(B, S, H, hd).transpose(0, 2, 1, 3)
    k = k.reshape(B, S, H, hd).transpose(0, 2, 1, 3)
    v = v.reshape(B, S, H, hd).transpose(0, 2, 1, 3)
    log_f = jax.nn.log_sigmoid((f_logit + b_f).astype(jnp.float32)).transpose(0, 2, 1)
    cum = jnp.cumsum(log_f, axis=-1)
    nb = S // Q_BLOCK
    qb = q.reshape(B, H, nb, Q_BLOCK, hd).transpose(2, 0, 1, 3, 4)
    cb = cum.reshape(B, H, nb, Q_BLOCK).transpose(2, 0, 1, 3)
    kpos = jnp.arange(S)
    scale = hd ** -0.5

    def block(args):
        i, q_i, c_i = args
        qpos = i * Q_BLOCK + jnp.arange(Q_BLOCK)
        logits = (jnp.einsum('bhqd,bhkd->bhqk', q_i, k).astype(jnp.float32) * scale
                  + c_i[..., :, None] - cum[..., None, :])
        logits = jnp.where(kpos[None, :] <= qpos[:, None], logits, -jnp.inf)
        p = jax.nn.softmax(logits, axis=-1)
        return jnp.einsum('bhqk,bhkd->bhqd', p.astype(v.dtype), v)

    ob = lax.map(block, (jnp.arange(nb), qb, cb))
    o = ob.transpose(1, 0, 3, 2, 4).reshape(B, S, D)
    return o @ w_out


def wkv7_scan(r, w, k, v, kk, a):
    B, S, H, N = r.shape
    nc = S // CHUNK

    def to_chunks(t):
        return t.reshape(B, nc, CHUNK, H, N).transpose(1, 2, 0, 3, 4)

    def step(state, inp):
        r_t, w_t, k_t, v_t, kk_t, a_t = inp
        sa = jnp.einsum('bhvk,bhk->bhv', state, kk_t)
        state = (state * w_t[:, :, None, :]
                 - sa[..., None] * (kk_t * a_t)[:, :, None, :]
                 + v_t[..., None] * k_t[:, :, None, :])
        y = jnp.einsum('bhvk,bhk->bhv', state, r_t)
        return state, y

    def chunk_step(state, inp):
        return lax.scan(step, state, inp)

    s0 = jnp.zeros((B, H, N, N), jnp.float32)
    xs = (to_chunks(r), to_chunks(w), to_chunks(k), to_chunks(v), to_chunks(kk), to_chunks(a))
    _, y = lax.scan(chunk_step, s0, xs)
    return y.transpose(2, 0, 1, 3, 4).reshape(B, S, H, N)


def rwkv7_time_mix(h, mu, w0, w1, w2, a0, a1, a2, g1, g2, k_k, k_a, r_k,
                   w_r, w_k, w_v, w_o, gn_w, gn_b):
    B, S, D = h.shape
    H, N = RWKV_HEADS, RWKV_HEAD_DIM
    xx = token_shift(h) - h
    mixed = h[None] + xx[None] * mu[:, None, None, :]
    xr, xw, xk, xv, xa, xg = mixed[0], mixed[1], mixed[2], mixed[3], mixed[4], mixed[5]
    r = xr @ w_r
    w_log = -jax.nn.softplus(-(w0 + jnp.tanh(xw @ w1) @ w2)) - 0.5
    decay = jnp.exp(-jnp.exp(w_log.astype(jnp.float32)))
    k = xk @ w_k
    v = xv @ w_v
    a = jax.nn.sigmoid(a0 + (xa @ a1) @ a2)
    g = jax.nn.sigmoid(xg @ g1) @ g2
    kk = k * k_k
    k = k * (1 + (a - 1) * k_a)

    def heads(t):
        return t.reshape(B, S, H, N).astype(jnp.float32)

    r_h, w_h, k_h, v_h, a_h = heads(r), heads(decay), heads(k), heads(v), heads(a)
    kk_h = heads(kk)
    kk_h = kk_h / jnp.maximum(jnp.sqrt(jnp.sum(kk_h * kk_h, axis=-1, keepdims=True)), 1e-12)
    y = wkv7_scan(r_h, w_h, k_h, v_h, kk_h, a_h)
    mean = jnp.mean(y, axis=-1, keepdims=True)
    var = jnp.mean(jnp.square(y - mean), axis=-1, keepdims=True)
    y = ((y - mean) * lax.rsqrt(var + GN_EPS)).reshape(B, S, D)
    y = y * gn_w.astype(jnp.float32) + gn_b.astype(jnp.float32)
    bonus = (jnp.sum(r_h * k_h * r_k.astype(jnp.float32), axis=-1, keepdims=True) * v_h).reshape(B, S, D)
    out = ((y + bonus).astype(h.dtype) * g)
    return out @ w_o


def conv_ffn(h, w_in, conv_w, conv_b, w_out):
    u = h @ w_in
    C = u.shape[-1]
    u = lax.conv_general_dilated(
        u, conv_w[:, None, :], window_strides=(1,), padding=[(CONV_WIDTH - 1, 0)],
        dimension_numbers=('NWC', 'WIO', 'NWC'), feature_group_count=C) + conv_b
    x1, x2 = jnp.split(u, 2, axis=-1)
    return (jax.nn.gelu(x1, approximate=False) * x2) @ w_out


def setup_inputs(seed: int = 0) -> dict:
    key = jax.random.key(seed)
    ks = jax.random.split(key, 32)
    D, F, H, N = D_MODEL, D_FF, FOX_HEADS, RWKV_HEAD_DIM
    nrm = lambda k, shape, s: jax.random.normal(k, shape, jnp.float32) * s
    return {
        "x": nrm(ks[0], (BATCH, SEQ, D), 1.0),
        "c": nrm(ks[1], (BATCH, D), 1.0),
        "mod_w": nrm(ks[2], (DEPTH, D, 6 * D), 0.5 * D ** -0.5),
        "mod_b": nrm(ks[3], (DEPTH, 6 * D), 0.01),
        "norm_g": 1.0 + nrm(ks[4], (DEPTH, 4, D), 0.05),
        "fox_w_in": nrm(ks[5], (N_FOX, D, 3 * D + H), D ** -0.5),
        "fox_b_f": jax.random.uniform(ks[6], (N_FOX, H), jnp.float32, 1.0, 5.0),
        "fox_w_out": nrm(ks[7], (N_FOX, D, D), D ** -0.5),
        "rwkv_mu": jax.random.uniform(ks[8], (N_RWKV, 6, D), jnp.float32),
        "rwkv_w0": jax.random.uniform(ks[9], (N_RWKV, D), jnp.float32, -6.0, -1.0),
        "rwkv_w1": nrm(ks[10], (N_RWKV, D, DECAY_LORA), D ** -0.5),
        "rwkv_w2": nrm(ks[11], (N_RWKV, DECAY_LORA, D), 0.1 * DECAY_LORA ** -0.5),
        "rwkv_a0": nrm(ks[12], (N_RWKV, D), 0.1),
        "rwkv_a1": nrm(ks[13], (N_RWKV, D, AAA_LORA), D ** -0.5),
        "rwkv_a2": nrm(ks[14], (N_RWKV, AAA_LORA, D), 0.1 * AAA_LORA ** -0.5),
        "rwkv_g1": nrm(ks[15], (N_RWKV, D, GATE_LORA), D ** -0.5),
        "rwkv_g2": nrm(ks[16], (N_RWKV, GATE_LORA, D), GATE_LORA ** -0.5),
        "rwkv_k_k": 0.85 + nrm(ks[17], (N_RWKV, D), 0.05),
        "rwkv_k_a": 1.0 + nrm(ks[18], (N_RWKV, D), 0.05),
        "rwkv_r_k": nrm(ks[19], (N_RWKV, RWKV_HEADS, N), 0.1),
        "rwkv_w_r": nrm(ks[20], (N_RWKV, D, D), D ** -0.5),
        "rwkv_w_k": nrm(ks[21], (N_RWKV, D, D), D ** -0.5),
        "rwkv_w_v": nrm(ks[22], (N_RWKV, D, D), D ** -0.5),
        "rwkv_w_o": nrm(ks[23], (N_RWKV, D, D), D ** -0.5),
        "rwkv_gn_w": 1.0 + nrm(ks[24], (N_RWKV, D), 0.05),
        "rwkv_gn_b": nrm(ks[25], (N_RWKV, D), 0.01),
        "ffn_w_in": nrm(ks[26], (DEPTH, D, 2 * F), D ** -0.5),
        "ffn_conv_w": nrm(ks[27], (DEPTH, CONV_WIDTH, 2 * F), CONV_WIDTH ** -0.5),
        "ffn_conv_b": nrm(ks[28], (DEPTH, 2 * F), 0.01),
        "ffn_w_out": nrm(ks[29], (DEPTH, F, D), F ** -0.5),
    }


def reference(x, c, mod_w, mod_b, norm_g, fox_w_in, fox_b_f, fox_w_out,
              rwkv_mu, rwkv_w0, rwkv_w1, rwkv_w2, rwkv_a0, rwkv_a1, rwkv_a2,
              rwkv_g1, rwkv_g2, rwkv_k_k, rwkv_k_a, rwkv_r_k,
              rwkv_w_r, rwkv_w_k, rwkv_w_v, rwkv_w_o, rwkv_gn_w, rwkv_gn_b,
              ffn_w_in, ffn_conv_w, ffn_conv_b, ffn_w_out):
    for i in range(DEPTH):
        mod = jax.nn.silu(c) @ mod_w[i] + mod_b[i]
        sh_m, sc_m, g_m, sh_f, sc_f, g_f = [m[:, None, :] for m in jnp.split(mod, 6, axis=-1)]

        h = rms_norm(x, norm_g[i, 0]) * (1 + sc_m) + sh_m
        j = i // N_MIXERS
        if i % N_MIXERS == 0:
            y = fox_attention(h, fox_w_in[j], fox_b_f[j], fox_w_out[j])
        else:
            y = rwkv7_time_mix(h, rwkv_mu[j], rwkv_w0[j], rwkv_w1[j], rwkv_w2[j],
                               rwkv_a0[j], rwkv_a1[j], rwkv_a2[j], rwkv_g1[j], rwkv_g2[j],
                               rwkv_k_k[j], rwkv_k_a[j], rwkv_r_k[j],
                               rwkv_w_r[j], rwkv_w_k[j], rwkv_w_v[j], rwkv_w_o[j],
                               rwkv_gn_w[j], rwkv_gn_b[j])
        x = x + g_m * rms_norm(y, norm_g[i, 1])

        h = rms_norm(x, norm_g[i, 2]) * (1 + sc_f) + sh_f
        y = conv_ffn(h, ffn_w_in[i], ffn_conv_w[i], ffn_conv_b[i], ffn_w_out[i])
        x = x + g_f * rms_norm(y, norm_g[i, 3])
    return x
```

```python
import functools

import jax
import jax.numpy as jnp
from jax import lax
from jax.experimental import pallas as pl
from jax.experimental.pallas import tpu as pltpu

F32 = jnp.float32
BF16 = jnp.bfloat16

NORM_EPS = 1e-6
GN_EPS = 64e-5
FOX_HEADS = 16
RWKV_HEAD_DIM = 64
CONV_WIDTH = 3

LANES = 128
SUBLANES = 8
MXU_DIM = 256
WKV_CHUNK = 64
WKV_HEADS_PER_GROUP = MXU_DIM // RWKV_HEAD_DIM
VMEM_LIMIT = 56 << 20


def _params(*sem):
    return pltpu.CompilerParams(dimension_semantics=sem, vmem_limit_bytes=VMEM_LIMIT)


def _rms(x, g):
    return x * lax.rsqrt(jnp.mean(x * x, axis=-1, keepdims=True) + NORM_EPS) * g


def _dot(a, b):
    return jnp.dot(a, b, preferred_element_type=F32)


def _dot_nt(a, b):
    return lax.dot_general(a, b, (((1,), (1,)), ((), ())), preferred_element_type=F32)


def _dot_tn(a, b):
    return lax.dot_general(a, b, (((0,), (0,)), ((), ())), preferred_element_type=F32)


def _split2(x):
    hi = x.astype(BF16)
    lo = (x - hi.astype(F32)).astype(BF16)
    return hi, lo


def _split3(x):
    hi = x.astype(BF16)
    r1 = x - hi.astype(F32)
    mid = r1.astype(BF16)
    lo = (r1 - mid.astype(F32)).astype(BF16)
    return hi, mid, lo


def _softplus(x):
    return jnp.maximum(x, 0.0) + jnp.log1p(jnp.exp(-jnp.abs(x)))


def _mod_kernel(c_ref, w_ref, b_ref, o_ref):
    c = c_ref[...]
    s = c * jax.nn.sigmoid(c)
    o_ref[0] = _dot(s.astype(BF16), w_ref[0].astype(BF16)) + b_ref[0]


def _modulation(c, mod_w, mod_b):
    depth, d, n = mod_w.shape
    b = c.shape[0]
    rows = -(-b // SUBLANES) * SUBLANES
    c_pad = jnp.pad(c, ((0, rows - b), (0, 0)))
    tn = 1024
    out = pl.pallas_call(
        _mod_kernel,
        grid=(depth, n // tn),
        in_specs=[pl.BlockSpec((rows, d), lambda l, j: (0, 0)),
                  pl.BlockSpec((1, d, tn), lambda l, j: (l, 0, j)),
                  pl.BlockSpec((1, 1, tn), lambda l, j: (l, 0, j))],
        out_specs=pl.BlockSpec((1, rows, tn), lambda l, j: (l, 0, j)),
        out_shape=jax.ShapeDtypeStruct((depth, rows, n), F32),
        compiler_params=_params("parallel", "parallel"),
        name="adaln_mod",
    )(c_pad, mod_w, mod_b.reshape(depth, 1, n))
    return out[:, :b].reshape(depth, b, 6, 1, d)


def _prenorm_kernel(x_ref, g_ref, sc_ref, sh_ref, o_ref):
    h = _rms(x_ref[...], g_ref[...]) * (1.0 + sc_ref[0]) + sh_ref[0]
    o_ref[...] = h.astype(o_ref.dtype)


def _prenorm(x2, g, sc, sh, seq, tm=512):
    m, d = x2.shape
    tpb = seq // tm
    return pl.pallas_call(
        _prenorm_kernel,
        grid=(m // tm,),
        in_specs=[pl.BlockSpec((tm, d), lambda i: (i, 0)),
                  pl.BlockSpec((1, d), lambda i: (0, 0)),
                  pl.BlockSpec((1, 1, d), lambda i: (i // tpb, 0, 0)),
                  pl.BlockSpec((1, 1, d), lambda i: (i // tpb, 0, 0))],
        out_specs=pl.BlockSpec((tm, d), lambda i: (i, 0)),
        out_shape=jax.ShapeDtypeStruct((m, d), BF16),
        compiler_params=_params("parallel"),
        name="prenorm",
    )(x2, g.reshape(1, d), sc, sh)


def _mm_kernel(a_ref, b_ref, o_ref):
    o_ref[...] = _dot(a_ref[...], b_ref[...]).astype(o_ref.dtype)


def _matmul(a, b, out_dtype, tm=1024, tn=512, name="matmul"):
    m, k = a.shape
    n = b.shape[1]
    tm, tn = min(tm, m), min(tn, n)
    return pl.pallas_call(
        _mm_kernel,
        grid=(m // tm, n // tn),
        in_specs=[pl.BlockSpec((tm, k), lambda i, j: (i, 0)),
                  pl.BlockSpec((k, tn), lambda i, j: (0, j))],
        out_specs=pl.BlockSpec((tm, tn), lambda i, j: (i, j)),
        out_shape=jax.ShapeDtypeStruct((m, n), out_dtype),
        compiler_params=_params("parallel", "parallel"),
        name=name,
    )(a, b)


def _out_kernel(*refs, nk, with_h):
    a_ref, w_ref, x_ref, g1_ref, gate_ref = refs[:5]
    if with_h:
        g2_ref, sc_ref, sh_ref, xo_ref, ho_ref = refs[5:10]
    else:
        xo_ref = refs[5]
    acc_ref = refs[-1] if nk > 1 else None

    def epilogue(y):
        xn = x_ref[...] + gate_ref[0] * _rms(y, g1_ref[...])
        xo_ref[...] = xn
        if with_h:
            h = _rms(xn, g2_ref[...]) * (1.0 + sc_ref[0]) + sh_ref[0]
            ho_ref[...] = h.astype(ho_ref.dtype)

    part = _dot(a_ref[...], w_ref[...])
    if nk == 1:
        epilogue(part)
        return
    k = pl.program_id(1)

    @pl.when(k == 0)
    def _():
        acc_ref[...] = part

    @pl.when(jnp.logical_and(k > 0, k < nk - 1))
    def _():
        acc_ref[...] += part

    @pl.when(k == nk - 1)
    def _():
        epilogue(acc_ref[...] + part)


def _matmul_postnorm(a, w, x2, g1, gate, seq, nxt=None, tm=512, tk=None):
    m, kdim = a.shape
    d = w.shape[1]
    tk = kdim if tk is None else tk
    nk = kdim // tk
    tpb = seq // tm
    with_h = nxt is not None
    row = lambda i, k: (i, 0)
    per_batch = lambda i, k: (i // tpb, 0, 0)
    const = lambda i, k: (0, 0)
    in_specs = [pl.BlockSpec((tm, tk), lambda i, k: (i, k)),
                pl.BlockSpec((tk, d), lambda i, k: (k, 0)),
                pl.BlockSpec((tm, d), row),
                pl.BlockSpec((1, d), const),
                pl.BlockSpec((1, 1, d), per_batch)]
    args = [a, w, x2, g1.reshape(1, d), gate]
    out_specs = [pl.BlockSpec((tm, d), row)]
    out_shape = [jax.ShapeDtypeStruct((m, d), F32)]
    if with_h:
        g2, sc, sh = nxt
        in_specs += [pl.BlockSpec((1, d), const), pl.BlockSpec((1, 1, d), per_batch),
                     pl.BlockSpec((1, 1, d), per_batch)]
        args += [g2.reshape(1, d), sc, sh]
        out_specs.append(pl.BlockSpec((tm, d), row))
        out_shape.append(jax.ShapeDtypeStruct((m, d), BF16))
    res = pl.pallas_call(
        functools.partial(_out_kernel, nk=nk, with_h=with_h),
        grid=(m // tm, nk),
        in_specs=in_specs,
        out_specs=out_specs,
        out_shape=out_shape,
        scratch_shapes=[pltpu.VMEM((tm, d), F32)] if nk > 1 else [],
        compiler_params=_params("parallel", "arbitrary"),
        name="matmul_postnorm",
    )(*args)
    return (res[0], res[1]) if with_h else (res[0], None)


def _fox_gate_kernel(h_ref, wf_ref, bf_ref, o_ref, carry_ref, *, ts):
    @pl.when(pl.program_id(1) == 0)
    def _():
        carry_ref[...] = jnp.zeros_like(carry_ref)

    logit = _dot(h_ref[...], wf_ref[...]) + bf_ref[...]
    log_f = -_softplus(-logit)
    row = lax.broadcasted_iota(jnp.int32, (ts, ts), 0)
    col = lax.broadcasted_iota(jnp.int32, (ts, ts), 1)
    tri = jnp.where(row >= col, 1.0, 0.0).astype(BF16)
    hi, mid, lo = _split3(log_f)
    cs = _dot(tri, hi) + _dot(tri, mid) + _dot(tri, lo) + carry_ref[...]
    o_ref[...] = cs
    carry_ref[...] = cs[ts - 1:ts, :]


def _fox_gate(h, wf_pad, bf_pad, batch, seq, ts=256):
    m, d = h.shape
    spb = seq // ts
    return pl.pallas_call(
        functools.partial(_fox_gate_kernel, ts=ts),
        grid=(batch, spb),
        in_specs=[pl.BlockSpec((ts, d), lambda b, s: (b * spb + s, 0)),
                  pl.BlockSpec((d, LANES), lambda b, s: (0, 0)),
                  pl.BlockSpec((1, LANES), lambda b, s: (0, 0))],
        out_specs=pl.BlockSpec((ts, LANES), lambda b, s: (b * spb + s, 0)),
        out_shape=jax.ShapeDtypeStruct((m, LANES), F32),
        scratch_shapes=[pltpu.VMEM((1, LANES), F32)],
        compiler_params=_params("parallel", "arbitrary"),
        name="fox_gate",
    )(h, wf_pad, bf_pad)


def _fox_attn_kernel(q_ref, k_ref, v_ref, cq_ref, ck_ref, o_ref, *, tq, scale):
    head = pl.program_id(1)
    i = pl.program_id(2)
    q = q_ref[...]
    lane = lax.broadcasted_iota(jnp.int32, (tq, LANES), 1)
    cq = jnp.sum(jnp.where(lane == head, cq_ref[...], 0.0), axis=1, keepdims=True)

    def tile(j, masked, carry):
        m_prev, l_prev, acc = carry
        start = pl.multiple_of(j * tq, tq)
        kj = k_ref[pl.ds(start, tq), :]
        vj = v_ref[pl.ds(start, tq), :]
        s = _dot_nt(q, kj) * scale + cq - ck_ref[0, 0, j]
        if masked:
            row = lax.broadcasted_iota(jnp.int32, (tq, tq), 0)
            col = lax.broadcasted_iota(jnp.int32, (tq, tq), 1)
            s = jnp.where(row >= col, s, -jnp.inf)
        m_new = jnp.maximum(m_prev, jnp.max(s, axis=1, keepdims=True))
        alpha = jnp.exp(m_prev - m_new)
        p = jnp.exp(s - m_new)
        l_new = alpha * l_prev + jnp.sum(p, axis=1, keepdims=True)
        acc = alpha * acc + _dot(p.astype(BF16), vj)
        return m_new, l_new, acc

    init = (jnp.full((tq, 1), -jnp.inf, F32), jnp.zeros((tq, 1), F32),
            jnp.zeros((tq, q.shape[1]), F32))
    carry = lax.fori_loop(0, i, lambda j, c: tile(j, False, c), init)
    _, l_fin, acc = tile(i, True, carry)
    o_ref[...] = (acc / l_fin).astype(o_ref.dtype)


def _fox_attention(qkv, cum, batch, seq, heads, tq=256):
    m = qkv.shape[0]
    hd = qkv.shape[1] // (3 * heads)
    nq = seq // tq
    ck = cum[:, :heads].reshape(batch, seq, heads).transpose(0, 2, 1).reshape(batch, heads, nq, 1, tq)
    return pl.pallas_call(
        functools.partial(_fox_attn_kernel, tq=tq, scale=hd ** -0.5),
        grid=(batch, heads, nq),
        in_specs=[pl.BlockSpec((tq, hd), lambda b, h, i: (b * nq + i, h)),
                  pl.BlockSpec((seq, hd), lambda b, h, i: (b, heads + h)),
                  pl.BlockSpec((seq, hd), lambda b, h, i: (b, 2 * heads + h)),
                  pl.BlockSpec((tq, LANES), lambda b, h, i: (b * nq + i, 0)),
                  pl.BlockSpec((1, 1, nq, 1, tq), lambda b, h, i: (b, h, 0, 0, 0))],
        out_specs=pl.BlockSpec((tq, hd), lambda b, h, i: (b * nq + i, h)),
        out_shape=jax.ShapeDtypeStruct((m, heads * hd), BF16),
        compiler_params=_params("parallel", "parallel", "arbitrary"),
        name="fox_attention",
    )(qkv, qkv, qkv, cum, ck)


def _ffn_in_kernel(h_ref, halo_ref, w1_ref, w2_ref, cw1_ref, cw2_ref, cb1_ref, cb2_ref, o_ref, *, tm, tpb):
    first = pl.program_id(0) % tpb == 0
    h = h_ref[...]
    halo = halo_ref[...]
    row = lax.broadcasted_iota(jnp.int32, (tm, 1), 0)

    def branch(w_ref, cw_ref, cb_ref):
        u = _dot(h, w_ref[...])
        uh = jnp.where(first, 0.0, _dot(halo, w_ref[...]))
        prev1 = jnp.where(row == 0, uh[SUBLANES - 1:SUBLANES], pltpu.roll(u, 1, 0))
        prev2 = jnp.where(row == 0, uh[SUBLANES - 2:SUBLANES - 1],
                          jnp.where(row == 1, uh[SUBLANES - 1:SUBLANES], pltpu.roll(u, 2, 0)))
        cw = cw_ref[...]
        return cw[0:1] * prev2 + cw[1:2] * prev1 + cw[2:3] * u + cb_ref[...]

    x1 = branch(w1_ref, cw1_ref, cb1_ref)
    x2 = branch(w2_ref, cw2_ref, cb2_ref)
    gelu = 0.5 * x1 * (1.0 + lax.erf(x1 * (2.0 ** -0.5)))
    o_ref[...] = (gelu * x2).astype(o_ref.dtype)


def _ffn_in(h, w_in, conv_w, conv_b, seq, tm=512, tn=512):
    m, d = h.shape
    f = w_in.shape[1] // 2
    nf = f // tn
    tpb = seq // tm
    hb = tm // SUBLANES
    return pl.pallas_call(
        functools.partial(_ffn_in_kernel, tm=tm, tpb=tpb),
        grid=(m // tm, nf),
        in_specs=[pl.BlockSpec((tm, d), lambda i, j: (i, 0)),
                  pl.BlockSpec((SUBLANES, d), lambda i, j: (jnp.maximum(i * hb - 1, 0), 0)),
                  pl.BlockSpec((d, tn), lambda i, j: (0, j)),
                  pl.BlockSpec((d, tn), lambda i, j: (0, nf + j)),
                  pl.BlockSpec((CONV_WIDTH, tn), lambda i, j: (0, j)),
                  pl.BlockSpec((CONV_WIDTH, tn), lambda i, j: (0, nf + j)),
                  pl.BlockSpec((1, tn), lambda i, j: (0, j)),
                  pl.BlockSpec((1, tn), lambda i, j: (0, nf + j))],
        out_specs=pl.BlockSpec((tm, tn), lambda i, j: (i, j)),
        out_shape=jax.ShapeDtypeStruct((m, f), BF16),
        compiler_params=_params("parallel", "parallel"),
        name="ffn_in",
    )(h, h, w_in, w_in, conv_w, conv_w, conv_b.reshape(1, -1), conv_b.reshape(1, -1))


def _rwkv_pre_kernel(x_ref, halo_ref, g_ref, sc_ref, sh_ref, mu_ref, *o_refs, tm, tpb):
    first = pl.program_id(0) % tpb == 0
    g = g_ref[...]
    mod = lambda t: _rms(t, g) * (1.0 + sc_ref[0]) + sh_ref[0]
    h = mod(x_ref[...])
    h_halo = jnp.where(first, 0.0, mod(halo_ref[...]))
    row = lax.broadcasted_iota(jnp.int32, (tm, 1), 0)
    prev = jnp.where(row == 0, h_halo[SUBLANES - 1:SUBLANES], pltpu.roll(h, 1, 0))
    xx = prev - h
    mu = mu_ref[...]
    for n, o_ref in enumerate(o_refs):
        o_ref[...] = (h + xx * mu[n:n + 1]).astype(o_ref.dtype)


def _rwkv_pre(x2, g, sc, sh, mu, seq, tm=256):
    m, d = x2.shape
    tpb = seq // tm
    hb = tm // SUBLANES
    nmix = mu.shape[0]
    return pl.pallas_call(
        functools.partial(_rwkv_pre_kernel, tm=tm, tpb=tpb),
        grid=(m // tm,),
        in_specs=[pl.BlockSpec((tm, d), lambda i: (i, 0)),
                  pl.BlockSpec((SUBLANES, d), lambda i: (jnp.maximum(i * hb - 1, 0), 0)),
                  pl.BlockSpec((1, d), lambda i: (0, 0)),
                  pl.BlockSpec((1, 1, d), lambda i: (i // tpb, 0, 0)),
                  pl.BlockSpec((1, 1, d), lambda i: (i // tpb, 0, 0)),
                  pl.BlockSpec((nmix, d), lambda i: (0, 0))],
        out_specs=[pl.BlockSpec((tm, d), lambda i: (i, 0))] * nmix,
        out_shape=[jax.ShapeDtypeStruct((m, d), BF16)] * nmix,
        compiler_params=_params("parallel"),
        name="rwkv_pre",
    )(x2, x2, g.reshape(1, d), sc, sh, mu)


def _lora_kernel(x_ref, w1_ref, w2_ref, o_ref, *, act):
    t = _dot(x_ref[...], w1_ref[...])
    if act == "tanh":
        t = jnp.tanh(t)
    elif act == "sigmoid":
        t = jax.nn.sigmoid(t)
    o_ref[...] = _dot(t.astype(BF16), w2_ref[...])


def _lora(x, w1, w2, act, tm=512):
    m, d = x.shape
    r = w1.shape[1]
    n = w2.shape[1]
    return pl.pallas_call(
        functools.partial(_lora_kernel, act=act),
        grid=(m // tm,),
        in_specs=[pl.BlockSpec((tm, d), lambda i: (i, 0)),
                  pl.BlockSpec((d, r), lambda i: (0, 0)),
                  pl.BlockSpec((r, n), lambda i: (0, 0))],
        out_specs=pl.BlockSpec((tm, n), lambda i: (i, 0)),
        out_shape=jax.ShapeDtypeStruct((m, n), F32),
        compiler_params=_params("parallel"),
        name="rwkv_lora_" + act,
    )(x, w1, w2)


def _wkv_kernel(r_ref, k_ref, v_ref, wl_ref, al_ref, g_ref,
                w0_ref, a0_ref, kk_ref, ka_ref, rk_ref, gnw_ref, gnb_ref,
                o_ref, s_ref, *, groups):
    L = WKV_CHUNK
    N = RWKV_HEAD_DIM
    W = MXU_DIM

    @pl.when(pl.program_id(2) == 0)
    def _():
        s_ref[...] = jnp.zeros_like(s_ref)

    ri = lax.broadcasted_iota(jnp.int32, (W, W), 0)
    ci = lax.broadcasted_iota(jnp.int32, (W, W), 1)
    bd = (ri // N) == (ci // N)
    bd_bf = jnp.where(bd, 1.0, 0.0).astype(BF16)
    t_i = lax.broadcasted_iota(jnp.int32, (L, W), 0)
    s_i = lax.broadcasted_iota(jnp.int32, (L, W), 1) % N
    strict = t_i > s_i
    incl = t_i >= s_i
    eye = jnp.where(t_i == s_i, 1.0, 0.0)
    tri = jnp.where(lax.broadcasted_iota(jnp.int32, (L, L), 0) >= lax.broadcasted_iota(jnp.int32, (L, L), 1),
                    1.0, 0.0).astype(BF16)

    def block_diag(x):
        xb = x.astype(BF16)
        return jnp.concatenate([xb] * (W // L), axis=0) * bd_bf

    def stack(a, b):
        return jnp.concatenate([a, b], axis=0).astype(BF16)

    def head_sum(x):
        hi, lo = _split2(x)
        res = _dot(jnp.concatenate([hi, lo], axis=0), bd_bf)
        return res[:L] + res[L:]

    for gi in range(groups):
        sl = slice(gi * W, (gi + 1) * W)
        r = r_ref[:, sl]
        k = k_ref[:, sl]
        v = v_ref[:, sl]
        w_log = -_softplus(-(w0_ref[:, sl] + wl_ref[:, sl])) - 0.5
        lw = -jnp.exp(w_log)
        a = jax.nn.sigmoid(a0_ref[:, sl] + al_ref[:, sl])
        kk = k * kk_ref[:, sl]
        k = k * (1.0 + (a - 1.0) * ka_ref[:, sl])
        kk = kk / jnp.maximum(jnp.sqrt(head_sum(kk * kk)), 1e-12)

        hi, mid, lo = _split3(lw)
        cum = _dot(tri, hi) + _dot(tri, mid) + _dot(tri, lo)
        cum_last = cum[L - 1:L]
        p = jnp.exp(cum)
        p_inv = jnp.exp(-cum)
        p_prev = jnp.exp(cum - lw)
        p_rest = jnp.exp(cum_last - cum)
        b = kk * a

        lhs_ar = stack(-kk * p_prev, r * p)
        ab = _dot_nt(lhs_ar, block_diag(b * p_inv))
        ak = _dot_nt(lhs_ar, block_diag(k * p_inv))
        a_ab = jnp.where(strict, ab[:L], 0.0)
        a_rb = jnp.where(incl, ab[L:], 0.0)
        a_ak = jnp.where(strict, ak[:L], 0.0)
        a_rk = jnp.where(incl, ak[L:], 0.0)

        inv = eye + a_ab
        npow = _dot(a_ab.astype(BF16), block_diag(a_ab))
        for _ in range(4):
            res = _dot(stack(npow, inv), block_diag(npow))
            npow = res[:L]
            inv = inv + res[L:]
        inv = inv + _dot(inv.astype(BF16), block_diag(npow))

        s0 = s_ref[gi]
        a_s = _dot_nt(lhs_ar, s0.astype(BF16))
        a_v = _dot(stack(a_ak, a_rk), block_diag(v))
        z = _dot(inv.astype(BF16), block_diag(a_v[:L] + a_s[:L]))
        y = a_s[L:] + a_v[L:] + _dot(a_rb.astype(BF16), block_diag(z))

        upd = _dot_tn(stack(z, v), stack(b * p_rest, k * p_rest))
        s_ref[gi] = jnp.where(bd, s0 * jnp.exp(cum_last) + upd, 0.0)

        mean = head_sum(y) * (1.0 / N)
        yc = y - mean
        var = head_sum(yc * yc) * (1.0 / N)
        yn = yc * lax.rsqrt(var + GN_EPS) * gnw_ref[:, sl] + gnb_ref[:, sl]
        bonus = head_sum(r * k * rk_ref[:, sl]) * v
        o_ref[:, sl] = ((yn + bonus) * g_ref[:, sl]).astype(o_ref.dtype)


def _wkv(r, k, v, wl, al, g, w0, a0, k_k, k_a, r_k, gn_w, gn_b, batch, seq, groups=4):
    m, d = r.shape
    L = WKV_CHUNK
    gw = groups * MXU_DIM
    ng = d // gw
    nc = seq // L
    tok = pl.BlockSpec((L, gw), lambda b, gidx, c: (b * nc + c, gidx))
    par = pl.BlockSpec((1, gw), lambda b, gidx, c: (0, gidx))
    row = lambda t: t.reshape(1, d)
    return pl.pallas_call(
        functools.partial(_wkv_kernel, groups=groups),
        grid=(batch, ng, nc),
        in_specs=[tok] * 6 + [par] * 7,
        out_specs=tok,
        out_shape=jax.ShapeDtypeStruct((m, d), BF16),
        scratch_shapes=[pltpu.VMEM((groups, MXU_DIM, MXU_DIM), F32)],
        compiler_params=_params("parallel", "parallel", "arbitrary"),
        name="wkv7",
    )(r, k, v, wl, al, g, row(w0), row(a0), row(k_k), row(k_a), row(r_k), row(gn_w), row(gn_b))


def _pad_to(x, axis, size):
    pad = [(0, 0)] * x.ndim
    pad[axis] = (0, size - x.shape[axis])
    return jnp.pad(x, pad)


def _round_up(n, mult):
    return -(-n // mult) * mult


def kernel(x, c, mod_w, mod_b, norm_g, fox_w_in, fox_b_f, fox_w_out, rwkv_mu, rwkv_w0, rwkv_w1, rwkv_w2, rwkv_a0, rwkv_a1, rwkv_a2, rwkv_g1, rwkv_g2, rwkv_k_k, rwkv_k_a, rwkv_r_k, rwkv_w_r, rwkv_w_k, rwkv_w_v, rwkv_w_o, rwkv_gn_w, rwkv_gn_b, ffn_w_in, ffn_conv_w, ffn_conv_b, ffn_w_out):
    batch, seq, d = x.shape
    depth = mod_w.shape[0]
    heads = fox_b_f.shape[1]
    bf = lambda t: t.astype(BF16)

    mod = _modulation(c, mod_w, mod_b)
    x2 = x.reshape(batch * seq, d)

    h = None
    for i in range(depth):
        sh_m, sc_m, g_m, sh_f, sc_f, g_f = [mod[i, :, n] for n in range(6)]
        j = i // 2
        ffn_pre = (norm_g[i, 2], sc_f, sh_f)
        if i % 2 == 0:
            if h is None:
                h = _prenorm(x2, norm_g[i, 0], sc_m, sh_m, seq)
            w_in = fox_w_in[j]
            qkv = _matmul(h, bf(w_in[:, :3 * d]), BF16, name="fox_qkv")
            wf = _pad_to(bf(w_in[:, 3 * d:]), 1, LANES)
            bfp = _pad_to(fox_b_f[j].reshape(1, heads), 1, LANES)
            cum = _fox_gate(h, wf, bfp, batch, seq)
            o = _fox_attention(qkv, cum, batch, seq, heads)
            x2, h = _matmul_postnorm(o, bf(fox_w_out[j]), x2, norm_g[i, 1], g_m, seq, nxt=ffn_pre)
        else:
            xr, xw, xk, xv, xa, xg = _rwkv_pre(x2, norm_g[i, 0], sc_m, sh_m, rwkv_mu[j], seq)
            r = _matmul(xr, bf(rwkv_w_r[j]), F32, name="rwkv_r")
            k = _matmul(xk, bf(rwkv_w_k[j]), F32, name="rwkv_k")
            v = _matmul(xv, bf(rwkv_w_v[j]), F32, name="rwkv_v")
            lr = _round_up(rwkv_w1.shape[2], LANES)
            wl = _lora(xw, _pad_to(bf(rwkv_w1[j]), 1, lr), _pad_to(bf(rwkv_w2[j]), 0, lr), "tanh")
            al = _lora(xa, _pad_to(bf(rwkv_a1[j]), 1, lr), _pad_to(bf(rwkv_a2[j]), 0, lr), "none")
            gr = _round_up(rwkv_g1.shape[2], LANES)
            g = _lora(xg, _pad_to(bf(rwkv_g1[j]), 1, gr), _pad_to(bf(rwkv_g2[j]), 0, gr), "sigmoid")
            o = _wkv(r, k, v, wl, al, g, rwkv_w0[j], rwkv_a0[j], rwkv_k_k[j], rwkv_k_a[j],
                     rwkv_r_k[j], rwkv_gn_w[j], rwkv_gn_b[j], batch, seq)
            x2, h = _matmul_postnorm(o, bf(rwkv_w_o[j]), x2, norm_g[i, 1], g_m, seq, nxt=ffn_pre)
        act = _ffn_in(h, bf(ffn_w_in[i]), ffn_conv_w[i], ffn_conv_b[i], seq)
        f = act.shape[1]
        nxt = None
        if i + 1 < depth and (i + 1) % 2 == 0:
            nxt = (norm_g[i + 1, 0], mod[i + 1, :, 1], mod[i + 1, :, 0])
        x2, h = _matmul_postnorm(act, bf(ffn_w_out[i]), x2, norm_g[i, 3], g_f, seq, nxt=nxt, tk=f // 4)
    return x2.reshape(batch, seq, d)
```

```python
import functools
import math

import jax
import jax.numpy as jnp
from jax import lax
from jax.experimental import pallas as pl
from jax.experimental.pallas import tpu as pltpu

F32 = jnp.float32
BF16 = jnp.bfloat16

NORM_EPS = 1e-6
GN_EPS = 64e-5
FOX_HEADS = 16
RWKV_HEAD_DIM = 64
CONV_WIDTH = 3

LANES = 128
SUBLANES = 8
MXU_DIM = 256
WKV_CHUNK = 64
WKV_HEADS_PER_GROUP = MXU_DIM // RWKV_HEAD_DIM
VMEM_LIMIT = 56 << 20


def _params(*sem):
    return pltpu.CompilerParams(dimension_semantics=sem, vmem_limit_bytes=VMEM_LIMIT)


def _rms(x, g):
    return x * lax.rsqrt(jnp.mean(x * x, axis=-1, keepdims=True) + NORM_EPS) * g


def _dot(a, b):
    return jnp.dot(a, b, preferred_element_type=F32)


def _dot_nt(a, b):
    return lax.dot_general(a, b, (((1,), (1,)), ((), ())), preferred_element_type=F32)


def _dot_tn(a, b):
    return lax.dot_general(a, b, (((0,), (0,)), ((), ())), preferred_element_type=F32)


def _split2(x):
    hi = x.astype(BF16)
    lo = (x - hi.astype(F32)).astype(BF16)
    return hi, lo


def _split3(x):
    hi = x.astype(BF16)
    r1 = x - hi.astype(F32)
    mid = r1.astype(BF16)
    lo = (r1 - mid.astype(F32)).astype(BF16)
    return hi, mid, lo


def _softplus(x):
    return jnp.maximum(x, 0.0) + jnp.log1p(jnp.exp(-jnp.abs(x)))


def _mod_kernel(c_ref, w_ref, b_ref, o_ref):
    c = c_ref[...]
    s = c * jax.nn.sigmoid(c)
    o_ref[0] = _dot(s.astype(BF16), w_ref[0].astype(BF16)) + b_ref[0]


def _modulation(c, mod_w, mod_b):
    depth, d, n = mod_w.shape
    b = c.shape[0]
    rows = -(-b // SUBLANES) * SUBLANES
    c_pad = jnp.pad(c, ((0, rows - b), (0, 0)))
    tn = 1024
    out = pl.pallas_call(
        _mod_kernel,
        grid=(depth, n // tn),
        in_specs=[pl.BlockSpec((rows, d), lambda l, j: (0, 0)),
                  pl.BlockSpec((1, d, tn), lambda l, j: (l, 0, j)),
                  pl.BlockSpec((1, 1, tn), lambda l, j: (l, 0, j))],
        out_specs=pl.BlockSpec((1, rows, tn), lambda l, j: (l, 0, j)),
        out_shape=jax.ShapeDtypeStruct((depth, rows, n), F32),
        compiler_params=_params("parallel", "parallel"),
        name="adaln_mod",
    )(c_pad, mod_w, mod_b.reshape(depth, 1, n))
    return out[:, :b].reshape(depth, b, 6, 1, d)


def _prenorm_kernel(x_ref, g_ref, sc_ref, sh_ref, o_ref):
    h = _rms(x_ref[...], g_ref[...]) * (1.0 + sc_ref[0]) + sh_ref[0]
    o_ref[...] = h.astype(o_ref.dtype)


def _prenorm(x2, g, sc, sh, seq, tm=512):
    m, d = x2.shape
    tpb = seq // tm
    return pl.pallas_call(
        _prenorm_kernel,
        grid=(m // tm,),
        in_specs=[pl.BlockSpec((tm, d), lambda i: (i, 0)),
                  pl.BlockSpec((1, d), lambda i: (0, 0)),
                  pl.BlockSpec((1, 1, d), lambda i: (i // tpb, 0, 0)),
                  pl.BlockSpec((1, 1, d), lambda i: (i // tpb, 0, 0))],
        out_specs=pl.BlockSpec((tm, d), lambda i: (i, 0)),
        out_shape=jax.ShapeDtypeStruct((m, d), BF16),
        compiler_params=_params("parallel"),
        name="prenorm",
    )(x2, g.reshape(1, d), sc, sh)


def _mm_kernel(a_ref, b_ref, o_ref):
    o_ref[...] = _dot(a_ref[...], b_ref[...]).astype(o_ref.dtype)


def _matmul(a, b, out_dtype, tm=1024, tn=512, name="matmul"):
    m, k = a.shape
    n = b.shape[1]
    tm, tn = min(tm, m), min(tn, n)
    return pl.pallas_call(
        _mm_kernel,
        grid=(m // tm, n // tn),
        in_specs=[pl.BlockSpec((tm, k), lambda i, j: (i, 0)),
                  pl.BlockSpec((k, tn), lambda i, j: (0, j))],
        out_specs=pl.BlockSpec((tm, tn), lambda i, j: (i, j)),
        out_shape=jax.ShapeDtypeStruct((m, n), out_dtype),
        compiler_params=_params("parallel", "parallel"),
        name=name,
    )(a, b)


def _out_kernel(*refs, nk, with_h):
    a_ref, w_ref, x_ref, g1_ref, gate_ref = refs[:5]
    if with_h:
        g2_ref, sc_ref, sh_ref, xo_ref, ho_ref = refs[5:10]
    else:
        xo_ref = refs[5]
    acc_ref = refs[-1] if nk > 1 else None

    def epilogue(y):
        xn = x_ref[...] + gate_ref[0] * _rms(y, g1_ref[...])
        xo_ref[...] = xn
        if with_h:
            h = _rms(xn, g2_ref[...]) * (1.0 + sc_ref[0]) + sh_ref[0]
            ho_ref[...] = h.astype(ho_ref.dtype)

    part = _dot(a_ref[...], w_ref[...])
    if nk == 1:
        epilogue(part)
        return
    k = pl.program_id(1)

    @pl.when(k == 0)
    def _():
        acc_ref[...] = part

    @pl.when(jnp.logical_and(k > 0, k < nk - 1))
    def _():
        acc_ref[...] += part

    @pl.when(k == nk - 1)
    def _():
        epilogue(acc_ref[...] + part)


def _matmul_postnorm(a, w, x2, g1, gate, seq, nxt=None, tm=512, tk=None):
    m, kdim = a.shape
    d = w.shape[1]
    tk = kdim if tk is None else tk
    nk = kdim // tk
    tpb = seq // tm
    with_h = nxt is not None
    row = lambda i, k: (i, 0)
    per_batch = lambda i, k: (i // tpb, 0, 0)
    const = lambda i, k: (0, 0)
    in_specs = [pl.BlockSpec((tm, tk), lambda i, k: (i, k)),
                pl.BlockSpec((tk, d), lambda i, k: (k, 0)),
                pl.BlockSpec((tm, d), row),
                pl.BlockSpec((1, d), const),
                pl.BlockSpec((1, 1, d), per_batch)]
    args = [a, w, x2, g1.reshape(1, d), gate]
    out_specs = [pl.BlockSpec((tm, d), row)]
    out_shape = [jax.ShapeDtypeStruct((m, d), F32)]
    if with_h:
        g2, sc, sh = nxt
        in_specs += [pl.BlockSpec((1, d), const), pl.BlockSpec((1, 1, d), per_batch),
                     pl.BlockSpec((1, 1, d), per_batch)]
        args += [g2.reshape(1, d), sc, sh]
        out_specs.append(pl.BlockSpec((tm, d), row))
        out_shape.append(jax.ShapeDtypeStruct((m, d), BF16))
    res = pl.pallas_call(
        functools.partial(_out_kernel, nk=nk, with_h=with_h),
        grid=(m // tm, nk),
        in_specs=in_specs,
        out_specs=out_specs,
        out_shape=out_shape,
        scratch_shapes=[pltpu.VMEM((tm, d), F32)] if nk > 1 else [],
        compiler_params=_params("parallel", "arbitrary"),
        name="matmul_postnorm",
    )(*args)
    return (res[0], res[1]) if with_h else (res[0], None)


def _fox_gate_kernel(h_ref, wf_ref, bf_ref, o_ref, carry_ref, *, ts):
    @pl.when(pl.program_id(1) == 0)
    def _():
        carry_ref[...] = jnp.zeros_like(carry_ref)

    logit = _dot(h_ref[...], wf_ref[...]) + bf_ref[...]
    log_f = -_softplus(-logit)
    row = lax.broadcasted_iota(jnp.int32, (ts, ts), 0)
    col = lax.broadcasted_iota(jnp.int32, (ts, ts), 1)
    tri = jnp.where(row >= col, 1.0, 0.0).astype(BF16)
    hi, mid, lo = _split3(log_f)
    cs = _dot(tri, hi) + _dot(tri, mid) + _dot(tri, lo) + carry_ref[...]
    o_ref[...] = cs
    carry_ref[...] = cs[ts - 1:ts, :]


def _fox_gate(h, wf_pad, bf_pad, batch, seq, ts=256):
    m, d = h.shape
    spb = seq // ts
    return pl.pallas_call(
        functools.partial(_fox_gate_kernel, ts=ts),
        grid=(batch, spb),
        in_specs=[pl.BlockSpec((ts, d), lambda b, s: (b * spb + s, 0)),
                  pl.BlockSpec((d, LANES), lambda b, s: (0, 0)),
                  pl.BlockSpec((1, LANES), lambda b, s: (0, 0))],
        out_specs=pl.BlockSpec((ts, LANES), lambda b, s: (b * spb + s, 0)),
        out_shape=jax.ShapeDtypeStruct((m, LANES), F32),
        scratch_shapes=[pltpu.VMEM((1, LANES), F32)],
        compiler_params=_params("parallel", "arbitrary"),
        name="fox_gate",
    )(h, wf_pad, bf_pad)


def _fox_attn_kernel(q_ref, k_ref, v_ref, cq_ref, ck_ref, o_ref, *, tq, tk, scale):
    head = pl.program_id(1)
    i = pl.program_id(2)
    ratio = tq // tk
    hd = q_ref.shape[1]
    lane = lax.broadcasted_iota(jnp.int32, (tq, LANES), 1)
    cq = jnp.sum(jnp.where(lane == head, cq_ref[...], 0.0), axis=1, keepdims=True)
    row_id = lax.broadcasted_iota(jnp.int32, (tk, tk), 0)
    col_id = lax.broadcasted_iota(jnp.int32, (tk, tk), 1)

    def update(j, row0, masked, carry):
        start = pl.multiple_of(j * tk, tk)
        kj = k_ref[pl.ds(start, tk), :]
        vj = v_ref[pl.ds(start, tk), :]
        ckj = ck_ref[0, 0, j]
        out = []
        for r0, r1 in [(row0, tq)]:
            m_prev, l_prev, acc = carry
            s = _dot_nt(q_ref[r0:r1, :], kj) * scale + cq[r0:r1] - ckj
            if masked:
                top = jnp.where(row_id >= col_id, s[:tk], -jnp.inf)
                s = top if r1 - r0 == tk else jnp.concatenate([top, s[tk:]], axis=0)
            m_new = jnp.maximum(m_prev, jnp.max(s, axis=1, keepdims=True))
            alpha = jnp.exp(m_prev - m_new)
            p = jnp.exp(s - m_new)
            l_new = alpha * l_prev + jnp.sum(p, axis=1, keepdims=True)
            out.append((m_new, l_new, alpha * acc + _dot(p.astype(BF16), vj)))
        return tuple(jnp.concatenate(xs, axis=0) for xs in zip(*out))

    init = (jnp.full((tq, 1), -jnp.inf, F32), jnp.zeros((tq, 1), F32), jnp.zeros((tq, hd), F32))
    carry = lax.fori_loop(0, ratio * i, lambda j, c: update(j, 0, False, c), init)
    done = []
    for t in range(ratio):
        carry = update(ratio * i + t, t * tk, True, carry)
        done.append(tuple(x[:tk] for x in carry))
        if t + 1 < ratio:
            carry = tuple(x[tk:] for x in carry)
    l_fin = jnp.concatenate([d[1] for d in done], axis=0)
    acc = jnp.concatenate([d[2] for d in done], axis=0)
    o_ref[...] = (acc / l_fin).astype(o_ref.dtype)


def _fox_attention(qkv, cum, batch, seq, heads, tq=2048, tk=256):
    m = qkv.shape[0]
    hd = qkv.shape[1] // (3 * heads)
    nq = seq // tq
    nk = seq // tk
    ck = cum[:, :heads].reshape(batch, seq, heads).transpose(0, 2, 1).reshape(batch, heads, nk, 1, tk)
    return pl.pallas_call(
        functools.partial(_fox_attn_kernel, tq=tq, tk=tk, scale=hd ** -0.5),
        grid=(batch, heads, nq),
        in_specs=[pl.BlockSpec((tq, hd), lambda b, h, i: (b * nq + i, h)),
                  pl.BlockSpec((seq, hd), lambda b, h, i: (b, heads + h)),
                  pl.BlockSpec((seq, hd), lambda b, h, i: (b, 2 * heads + h)),
                  pl.BlockSpec((tq, LANES), lambda b, h, i: (b * nq + i, 0)),
                  pl.BlockSpec((1, 1, nk, 1, tk), lambda b, h, i: (b, h, 0, 0, 0))],
        out_specs=pl.BlockSpec((tq, hd), lambda b, h, i: (b * nq + i, h)),
        out_shape=jax.ShapeDtypeStruct((m, heads * hd), BF16),
        compiler_params=_params("parallel", "parallel", "arbitrary"),
        name="fox_attention",
    )(qkv, qkv, qkv, cum, ck)


def _ffn_in_kernel(h_ref, w1_ref, w2_ref, cw1_ref, cw2_ref, cb1_ref, cb2_ref, o_ref, tail_ref, *, tm, tpb):
    @pl.when(pl.program_id(1) % tpb == 0)
    def _():
        tail_ref[...] = jnp.zeros_like(tail_ref)

    h = h_ref[...]
    row = lax.broadcasted_iota(jnp.int32, (tm, 1), 0)

    def branch(n, w_ref, cw_ref, cb_ref):
        u = _dot(h, w_ref[...])
        tail = tail_ref[n]
        prev1 = jnp.where(row == 0, tail[SUBLANES - 1:SUBLANES], pltpu.roll(u, 1, 0))
        prev2 = jnp.where(row == 0, tail[SUBLANES - 2:SUBLANES - 1],
                          jnp.where(row == 1, tail[SUBLANES - 1:SUBLANES], pltpu.roll(u, 2, 0)))
        tail_ref[n] = u[tm - SUBLANES:]
        cw = cw_ref[...]
        return cw[0:1] * prev2 + cw[1:2] * prev1 + cw[2:3] * u + cb_ref[...]

    x1 = branch(0, w1_ref, cw1_ref, cb1_ref)
    x2 = branch(1, w2_ref, cw2_ref, cb2_ref)
    gelu = 0.5 * x1 * (1.0 + lax.erf(x1 * (2.0 ** -0.5)))
    o_ref[...] = (gelu * x2).astype(o_ref.dtype)


def _ffn_in(h, w_in, conv_w, conv_b, seq, tm=512, tn=512):
    m, d = h.shape
    f = w_in.shape[1] // 2
    nf = f // tn
    tpb = seq // tm
    return pl.pallas_call(
        functools.partial(_ffn_in_kernel, tm=tm, tpb=tpb),
        grid=(nf, m // tm),
        in_specs=[pl.BlockSpec((tm, d), lambda j, i: (i, 0)),
                  pl.BlockSpec((d, tn), lambda j, i: (0, j)),
                  pl.BlockSpec((d, tn), lambda j, i: (0, nf + j)),
                  pl.BlockSpec((CONV_WIDTH, tn), lambda j, i: (0, j)),
                  pl.BlockSpec((CONV_WIDTH, tn), lambda j, i: (0, nf + j)),
                  pl.BlockSpec((1, tn), lambda j, i: (0, j)),
                  pl.BlockSpec((1, tn), lambda j, i: (0, nf + j))],
        out_specs=pl.BlockSpec((tm, tn), lambda j, i: (i, j)),
        out_shape=jax.ShapeDtypeStruct((m, f), BF16),
        scratch_shapes=[pltpu.VMEM((2, SUBLANES, tn), F32)],
        compiler_params=_params("parallel", "arbitrary"),
        name="ffn_in",
    )(h, w_in, w_in, conv_w, conv_w, conv_b.reshape(1, -1), conv_b.reshape(1, -1))


def _rwkv_pre_kernel(x_ref, halo_ref, g_ref, sc_ref, sh_ref, mu_ref, *o_refs, tm, tpb):
    first = pl.program_id(0) % tpb == 0
    g = g_ref[...]
    mod = lambda t: _rms(t, g) * (1.0 + sc_ref[0]) + sh_ref[0]
    h = mod(x_ref[...])
    h_halo = jnp.where(first, 0.0, mod(halo_ref[...]))
    row = lax.broadcasted_iota(jnp.int32, (tm, 1), 0)
    prev = jnp.where(row == 0, h_halo[SUBLANES - 1:SUBLANES], pltpu.roll(h, 1, 0))
    xx = prev - h
    mu = mu_ref[...]
    for n, o_ref in enumerate(o_refs):
        o_ref[...] = (h + xx * mu[n:n + 1]).astype(o_ref.dtype)


def _rwkv_pre(x2, g, sc, sh, mu, seq, tm=256):
    m, d = x2.shape
    tpb = seq // tm
    hb = tm // SUBLANES
    nmix = mu.shape[0]
    return pl.pallas_call(
        functools.partial(_rwkv_pre_kernel, tm=tm, tpb=tpb),
        grid=(m // tm,),
        in_specs=[pl.BlockSpec((tm, d), lambda i: (i, 0)),
                  pl.BlockSpec((SUBLANES, d), lambda i: (jnp.maximum(i * hb - 1, 0), 0)),
                  pl.BlockSpec((1, d), lambda i: (0, 0)),
                  pl.BlockSpec((1, 1, d), lambda i: (i // tpb, 0, 0)),
                  pl.BlockSpec((1, 1, d), lambda i: (i // tpb, 0, 0)),
                  pl.BlockSpec((nmix, d), lambda i: (0, 0))],
        out_specs=[pl.BlockSpec((tm, d), lambda i: (i, 0))] * nmix,
        out_shape=[jax.ShapeDtypeStruct((m, d), BF16)] * nmix,
        compiler_params=_params("parallel"),
        name="rwkv_pre",
    )(x2, x2, g.reshape(1, d), sc, sh, mu)


def _lora_kernel(x_ref, w1_ref, w2_ref, o_ref, *, act):
    t = _dot(x_ref[...], w1_ref[...])
    if act == "tanh":
        t = jnp.tanh(t)
    elif act == "sigmoid":
        t = jax.nn.sigmoid(t)
    o_ref[...] = _dot(t.astype(BF16), w2_ref[...])


def _lora(x, w1, w2, act, tm=512):
    m, d = x.shape
    r = w1.shape[1]
    n = w2.shape[1]
    return pl.pallas_call(
        functools.partial(_lora_kernel, act=act),
        grid=(m // tm,),
        in_specs=[pl.BlockSpec((tm, d), lambda i: (i, 0)),
                  pl.BlockSpec((d, r), lambda i: (0, 0)),
                  pl.BlockSpec((r, n), lambda i: (0, 0))],
        out_specs=pl.BlockSpec((tm, n), lambda i: (i, 0)),
        out_shape=jax.ShapeDtypeStruct((m, n), F32),
        compiler_params=_params("parallel"),
        name="rwkv_lora_" + act,
    )(x, w1, w2)


def _wkv_kernel(r_ref, k_ref, v_ref, wl_ref, al_ref, g_ref,
                w0_ref, a0_ref, kk_ref, ka_ref, rk_ref, gnw_ref, gnb_ref,
                o_ref, s_ref, *, groups):
    L = WKV_CHUNK
    N = RWKV_HEAD_DIM
    W = MXU_DIM

    @pl.when(pl.program_id(2) == 0)
    def _():
        s_ref[...] = jnp.zeros_like(s_ref)

    ri = lax.broadcasted_iota(jnp.int32, (W, W), 0)
    ci = lax.broadcasted_iota(jnp.int32, (W, W), 1)
    bd = (ri // N) == (ci // N)
    bd_bf = jnp.where(bd, 1.0, 0.0).astype(BF16)
    t_i = lax.broadcasted_iota(jnp.int32, (L, W), 0)
    s_i = lax.broadcasted_iota(jnp.int32, (L, W), 1) % N
    strict = t_i > s_i
    incl = t_i >= s_i
    eye = jnp.where(t_i == s_i, 1.0, 0.0)
    tri = jnp.where(lax.broadcasted_iota(jnp.int32, (L, L), 0) >= lax.broadcasted_iota(jnp.int32, (L, L), 1),
                    1.0, 0.0).astype(BF16)

    def block_diag(x):
        xb = x.astype(BF16)
        return jnp.concatenate([xb] * (W // L), axis=0) * bd_bf

    def stack(a, b):
        return jnp.concatenate([a, b], axis=0).astype(BF16)

    def head_sum(x):
        hi, lo = _split2(x)
        res = _dot(jnp.concatenate([hi, lo], axis=0), bd_bf)
        return res[:L] + res[L:]

    sls = [slice(gi * W, (gi + 1) * W) for gi in range(groups)]

    def each(fn, *cols):
        return [fn(*args) for args in zip(*cols)]

    r = [r_ref[:, sl] for sl in sls]
    v = [v_ref[:, sl] for sl in sls]
    k_raw = [k_ref[:, sl] for sl in sls]
    lw = [-math.exp(-0.5) * jax.nn.sigmoid(w0_ref[:, sl] + wl_ref[:, sl]) for sl in sls]
    a = [jax.nn.sigmoid(a0_ref[:, sl] + al_ref[:, sl]) for sl in sls]
    kk = [kx * kk_ref[:, sl] for kx, sl in zip(k_raw, sls)]
    k = [kx * (1.0 + (ax - 1.0) * ka_ref[:, sl]) for kx, ax, sl in zip(k_raw, a, sls)]
    kk_sq = each(lambda x: head_sum(x * x), kk)
    kk = each(lambda x, sq: x / jnp.maximum(jnp.sqrt(sq), 1e-12), kk, kk_sq)

    def cumsum(x):
        hi, mid, lo = _split3(x)
        return _dot(tri, hi) + _dot(tri, mid) + _dot(tri, lo)

    cum = each(cumsum, lw)
    cum_last = [cx[L - 1:L] for cx in cum]
    p_inv = each(lambda cx: jnp.exp(-cx), cum)
    p_rest = each(lambda cx, cl: jnp.exp(cl - cx), cum, cum_last)
    b = each(lambda x, ax: x * ax, kk, a)
    lhs_ar = each(lambda x, cx, lx, rx: stack(-x * jnp.exp(cx - lx), rx * jnp.exp(cx)), kk, cum, lw, r)
    ab = each(lambda lhs, bx, pi: _dot_nt(lhs, block_diag(bx * pi)), lhs_ar, b, p_inv)
    ak = each(lambda lhs, kx, pi: _dot_nt(lhs, block_diag(kx * pi)), lhs_ar, k, p_inv)
    a_ab = [jnp.where(strict, x[:L], 0.0) for x in ab]
    a_rb = [jnp.where(incl, x[L:], 0.0) for x in ab]
    a_ak = [jnp.where(strict, x[:L], 0.0) for x in ak]
    a_rk = [jnp.where(incl, x[L:], 0.0) for x in ak]

    inv = [eye + x for x in a_ab]
    npow = each(lambda x: _dot(x.astype(BF16), block_diag(x)), a_ab)
    for _ in range(4):
        res = each(lambda n, t: _dot(stack(n, t), block_diag(n)), npow, inv)
        npow = [x[:L] for x in res]
        inv = [t + x[L:] for t, x in zip(inv, res)]
    inv = each(lambda t, n: t + _dot(t.astype(BF16), block_diag(n)), inv, npow)

    s0 = [s_ref[gi] for gi in range(groups)]
    a_s = each(lambda lhs, s: _dot_nt(lhs, s.astype(BF16)), lhs_ar, s0)
    a_v = each(lambda x, y, vx: _dot(stack(x, y), block_diag(vx)), a_ak, a_rk, v)
    z = each(lambda t, av, as_: _dot(t.astype(BF16), block_diag(av[:L] + as_[:L])), inv, a_v, a_s)
    y = each(lambda as_, av, arb, zx: as_[L:] + av[L:] + _dot(arb.astype(BF16), block_diag(zx)),
             a_s, a_v, a_rb, z)
    upd = each(lambda zx, vx, bx, kx, pr: _dot_tn(stack(zx, vx), stack(bx * pr, kx * pr)), z, v, b, k, p_rest)
    for gi in range(groups):
        s_ref[gi] = jnp.where(bd, s0[gi] * jnp.exp(cum_last[gi]) + upd[gi], 0.0)

    mean = each(lambda yx: head_sum(yx) * (1.0 / N), y)
    yc = each(lambda yx, mx: yx - mx, y, mean)
    var = each(lambda x: head_sum(x * x) * (1.0 / N), yc)
    bonus = [head_sum(rx * kx * rk_ref[:, sl]) * vx for rx, kx, vx, sl in zip(r, k, v, sls)]
    for gi, sl in enumerate(sls):
        yn = yc[gi] * lax.rsqrt(var[gi] + GN_EPS) * gnw_ref[:, sl] + gnb_ref[:, sl]
        o_ref[:, sl] = ((yn + bonus[gi]) * g_ref[:, sl]).astype(o_ref.dtype)


def _wkv(r, k, v, wl, al, g, w0, a0, k_k, k_a, r_k, gn_w, gn_b, batch, seq, groups=8):
    m, d = r.shape
    L = WKV_CHUNK
    gw = groups * MXU_DIM
    ng = d // gw
    nc = seq // L
    tok = pl.BlockSpec((L, gw), lambda b, gidx, c: (b * nc + c, gidx))
    par = pl.BlockSpec((1, gw), lambda b, gidx, c: (0, gidx))
    row = lambda t: t.reshape(1, d)
    return pl.pallas_call(
        functools.partial(_wkv_kernel, groups=groups),
        grid=(batch, ng, nc),
        in_specs=[tok] * 6 + [par] * 7,
        out_specs=tok,
        out_shape=jax.ShapeDtypeStruct((m, d), BF16),
        scratch_shapes=[pltpu.VMEM((groups, MXU_DIM, MXU_DIM), F32)],
        compiler_params=_params("parallel", "parallel", "arbitrary"),
        name="wkv7",
    )(r, k, v, wl, al, g, row(w0), row(a0), row(k_k), row(k_a), row(r_k), row(gn_w), row(gn_b))


def _pad_to(x, axis, size):
    pad = [(0, 0)] * x.ndim
    pad[axis] = (0, size - x.shape[axis])
    return jnp.pad(x, pad)


def _round_up(n, mult):
    return -(-n // mult) * mult


def kernel(x, c, mod_w, mod_b, norm_g, fox_w_in, fox_b_f, fox_w_out, rwkv_mu, rwkv_w0, rwkv_w1, rwkv_w2, rwkv_a0, rwkv_a1, rwkv_a2, rwkv_g1, rwkv_g2, rwkv_k_k, rwkv_k_a, rwkv_r_k, rwkv_w_r, rwkv_w_k, rwkv_w_v, rwkv_w_o, rwkv_gn_w, rwkv_gn_b, ffn_w_in, ffn_conv_w, ffn_conv_b, ffn_w_out):
    batch, seq, d = x.shape
    depth = mod_w.shape[0]
    heads = fox_b_f.shape[1]
    bf = lambda t: t.astype(BF16)

    mod = _modulation(c, mod_w, mod_b)
    x2 = x.reshape(batch * seq, d)

    h = None
    for i in range(depth):
        sh_m, sc_m, g_m, sh_f, sc_f, g_f = [mod[i, :, n] for n in range(6)]
        j = i // 2
        ffn_pre = (norm_g[i, 2], sc_f, sh_f)
        if i % 2 == 0:
            if h is None:
                h = _prenorm(x2, norm_g[i, 0], sc_m, sh_m, seq)
            w_in = fox_w_in[j]
            qkv = _matmul(h, bf(w_in[:, :3 * d]), BF16, name="fox_qkv")
            wf = _pad_to(bf(w_in[:, 3 * d:]), 1, LANES)
            bfp = _pad_to(fox_b_f[j].reshape(1, heads), 1, LANES)
            cum = _fox_gate(h, wf, bfp, batch, seq)
            o = _fox_attention(qkv, cum, batch, seq, heads)
            x2, h = _matmul_postnorm(o, bf(fox_w_out[j]), x2, norm_g[i, 1], g_m, seq, nxt=ffn_pre)
        else:
            xr, xw, xk, xv, xa, xg = _rwkv_pre(x2, norm_g[i, 0], sc_m, sh_m, rwkv_mu[j], seq)
            r = _matmul(xr, bf(rwkv_w_r[j]), F32, name="rwkv_r")
            k = _matmul(xk, bf(rwkv_w_k[j]), F32, name="rwkv_k")
            v = _matmul(xv, bf(rwkv_w_v[j]), F32, name="rwkv_v")
            lr = _round_up(rwkv_w1.shape[2], LANES)
            wl = _lora(xw, _pad_to(bf(rwkv_w1[j]), 1, lr), _pad_to(bf(rwkv_w2[j]), 0, lr), "tanh")
            al = _lora(xa, _pad_to(bf(rwkv_a1[j]), 1, lr), _pad_to(bf(rwkv_a2[j]), 0, lr), "none")
            gr = _round_up(rwkv_g1.shape[2], LANES)
            g = _lora(xg, _pad_to(bf(rwkv_g1[j]), 1, gr), _pad_to(bf(rwkv_g2[j]), 0, gr), "sigmoid")
            o = _wkv(r, k, v, wl, al, g, rwkv_w0[j], rwkv_a0[j], rwkv_k_k[j], rwkv_k_a[j],
                     rwkv_r_k[j], rwkv_gn_w[j], rwkv_gn_b[j], batch, seq)
            x2, h = _matmul_postnorm(o, bf(rwkv_w_o[j]), x2, norm_g[i, 1], g_m, seq, nxt=ffn_pre)
        act = _ffn_in(h, bf(ffn_w_in[i]), ffn_conv_w[i], ffn_conv_b[i], seq)
        f = act.shape[1]
        nxt = None
        if i + 1 < depth and (i + 1) % 2 == 0:
            nxt = (norm_g[i + 1, 0], mod[i + 1, :, 1], mod[i + 1, :, 0])
        x2, h = _matmul_postnorm(act, bf(ffn_w_out[i]), x2, norm_g[i, 3], g_f, seq, nxt=nxt, tk=f // 4)
    return x2.reshape(batch, seq, d)
```

```python
import functools
import math

import jax
import jax.numpy as jnp
from jax import lax
from jax.experimental import pallas as pl
from jax.experimental.pallas import tpu as pltpu

F32 = jnp.float32
BF16 = jnp.bfloat16

NORM_EPS = 1e-6
GN_EPS = 64e-5
FOX_HEADS = 16
RWKV_HEAD_DIM = 64
CONV_WIDTH = 3

LANES = 128
SUBLANES = 8
MXU_DIM = 256
WKV_CHUNK = 64
WKV_HEADS_PER_GROUP = MXU_DIM // RWKV_HEAD_DIM
VMEM_LIMIT = 56 << 20


def _params(*sem):
    return pltpu.CompilerParams(dimension_semantics=sem, vmem_limit_bytes=VMEM_LIMIT)


def _rms(x, g):
    return x * lax.rsqrt(jnp.mean(x * x, axis=-1, keepdims=True) + NORM_EPS) * g


def _dot(a, b):
    return jnp.dot(a, b, preferred_element_type=F32)


def _dot_nt(a, b):
    return lax.dot_general(a, b, (((1,), (1,)), ((), ())), preferred_element_type=F32)


def _dot_tn(a, b):
    return lax.dot_general(a, b, (((0,), (0,)), ((), ())), preferred_element_type=F32)


def _split2(x):
    hi = x.astype(BF16)
    lo = (x - hi.astype(F32)).astype(BF16)
    return hi, lo


def _split3(x):
    hi = x.astype(BF16)
    r1 = x - hi.astype(F32)
    mid = r1.astype(BF16)
    lo = (r1 - mid.astype(F32)).astype(BF16)
    return hi, mid, lo


def _softplus(x):
    return jnp.maximum(x, 0.0) + jnp.log1p(jnp.exp(-jnp.abs(x)))


def _mod_kernel(c_ref, w_ref, b_ref, o_ref):
    c = c_ref[...]
    s = c * jax.nn.sigmoid(c)
    o_ref[0] = _dot(s.astype(BF16), w_ref[0].astype(BF16)) + b_ref[0]


def _modulation(c, mod_w, mod_b):
    depth, d, n = mod_w.shape
    b = c.shape[0]
    rows = -(-b // SUBLANES) * SUBLANES
    c_pad = jnp.pad(c, ((0, rows - b), (0, 0)))
    tn = 1024
    out = pl.pallas_call(
        _mod_kernel,
        grid=(depth, n // tn),
        in_specs=[pl.BlockSpec((rows, d), lambda l, j: (0, 0)),
                  pl.BlockSpec((1, d, tn), lambda l, j: (l, 0, j)),
                  pl.BlockSpec((1, 1, tn), lambda l, j: (l, 0, j))],
        out_specs=pl.BlockSpec((1, rows, tn), lambda l, j: (l, 0, j)),
        out_shape=jax.ShapeDtypeStruct((depth, rows, n), F32),
        compiler_params=_params("parallel", "parallel"),
        name="adaln_mod",
    )(c_pad, mod_w, mod_b.reshape(depth, 1, n))
    return out[:, :b].reshape(depth, b, 6, 1, d)


def _prenorm_kernel(x_ref, g_ref, sc_ref, sh_ref, o_ref):
    h = _rms(x_ref[...], g_ref[...]) * (1.0 + sc_ref[0]) + sh_ref[0]
    o_ref[...] = h.astype(o_ref.dtype)


def _prenorm(x2, g, sc, sh, seq, tm=512):
    m, d = x2.shape
    tpb = seq // tm
    return pl.pallas_call(
        _prenorm_kernel,
        grid=(m // tm,),
        in_specs=[pl.BlockSpec((tm, d), lambda i: (i, 0)),
                  pl.BlockSpec((1, d), lambda i: (0, 0)),
                  pl.BlockSpec((1, 1, d), lambda i: (i // tpb, 0, 0)),
                  pl.BlockSpec((1, 1, d), lambda i: (i // tpb, 0, 0))],
        out_specs=pl.BlockSpec((tm, d), lambda i: (i, 0)),
        out_shape=jax.ShapeDtypeStruct((m, d), BF16),
        compiler_params=_params("parallel"),
        name="prenorm",
    )(x2, g.reshape(1, d), sc, sh)


def _mm_kernel(a_ref, w_ref, o_ref, wb_ref):
    @pl.when(pl.program_id(1) == 0)
    def _():
        wb_ref[...] = w_ref[...].astype(BF16)

    o_ref[...] = _dot(a_ref[...], wb_ref[...]).astype(o_ref.dtype)


def _matmul(a, w, layer, n, out_dtype, tm=1024, tn=512, name="matmul"):
    m, k = a.shape
    return pl.pallas_call(
        _mm_kernel,
        grid=(n // tn, m // tm),
        in_specs=[pl.BlockSpec((tm, k), lambda j, i: (i, 0)),
                  pl.BlockSpec((None, k, tn), lambda j, i: (layer, 0, j))],
        out_specs=pl.BlockSpec((tm, tn), lambda j, i: (i, j)),
        out_shape=jax.ShapeDtypeStruct((m, n), out_dtype),
        scratch_shapes=[pltpu.VMEM((k, tn), BF16)],
        compiler_params=_params("parallel", "arbitrary"),
        name=name,
    )(a, w)


def _out_kernel(*refs, nk, with_h):
    a_ref, w_ref, x_ref, g1_ref, gate_ref = refs[:5]
    if with_h:
        g2_ref, sc_ref, sh_ref, xo_ref, ho_ref = refs[5:10]
    else:
        xo_ref = refs[5]
    acc_ref = refs[-1] if nk > 1 else None

    def epilogue(y):
        xn = x_ref[...] + gate_ref[0] * _rms(y, g1_ref[...])
        xo_ref[...] = xn
        if with_h:
            h = _rms(xn, g2_ref[...]) * (1.0 + sc_ref[0]) + sh_ref[0]
            ho_ref[...] = h.astype(ho_ref.dtype)

    part = _dot(a_ref[...], w_ref[...])
    if nk == 1:
        epilogue(part)
        return
    k = pl.program_id(1)

    @pl.when(k == 0)
    def _():
        acc_ref[...] = part

    @pl.when(jnp.logical_and(k > 0, k < nk - 1))
    def _():
        acc_ref[...] += part

    @pl.when(k == nk - 1)
    def _():
        epilogue(acc_ref[...] + part)


def _matmul_postnorm(a, w, x2, g1, gate, seq, nxt=None, tm=512, tk=None):
    m, kdim = a.shape
    d = w.shape[1]
    tk = kdim if tk is None else tk
    nk = kdim // tk
    tpb = seq // tm
    with_h = nxt is not None
    row = lambda i, k: (i, 0)
    per_batch = lambda i, k: (i // tpb, 0, 0)
    const = lambda i, k: (0, 0)
    in_specs = [pl.BlockSpec((tm, tk), lambda i, k: (i, k)),
                pl.BlockSpec((tk, d), lambda i, k: (k, 0)),
                pl.BlockSpec((tm, d), row),
                pl.BlockSpec((1, d), const),
                pl.BlockSpec((1, 1, d), per_batch)]
    args = [a, w, x2, g1.reshape(1, d), gate]
    out_specs = [pl.BlockSpec((tm, d), row)]
    out_shape = [jax.ShapeDtypeStruct((m, d), F32)]
    if with_h:
        g2, sc, sh = nxt
        in_specs += [pl.BlockSpec((1, d), const), pl.BlockSpec((1, 1, d), per_batch),
                     pl.BlockSpec((1, 1, d), per_batch)]
        args += [g2.reshape(1, d), sc, sh]
        out_specs.append(pl.BlockSpec((tm, d), row))
        out_shape.append(jax.ShapeDtypeStruct((m, d), BF16))
    res = pl.pallas_call(
        functools.partial(_out_kernel, nk=nk, with_h=with_h),
        grid=(m // tm, nk),
        in_specs=in_specs,
        out_specs=out_specs,
        out_shape=out_shape,
        scratch_shapes=[pltpu.VMEM((tm, d), F32)] if nk > 1 else [],
        compiler_params=_params("parallel", "arbitrary"),
        name="matmul_postnorm",
    )(*args)
    return (res[0], res[1]) if with_h else (res[0], None)


def _fox_gate_kernel(h_ref, wf_ref, bf_ref, o_ref, carry_ref, *, ts):
    @pl.when(pl.program_id(1) == 0)
    def _():
        carry_ref[...] = jnp.zeros_like(carry_ref)

    logit = _dot(h_ref[...], wf_ref[...]) + bf_ref[...]
    log_f = -_softplus(-logit)
    row = lax.broadcasted_iota(jnp.int32, (ts, ts), 0)
    col = lax.broadcasted_iota(jnp.int32, (ts, ts), 1)
    tri = jnp.where(row >= col, 1.0, 0.0).astype(BF16)
    hi, mid, lo = _split3(log_f)
    cs = _dot(tri, hi) + _dot(tri, mid) + _dot(tri, lo) + carry_ref[...]
    o_ref[...] = cs
    carry_ref[...] = cs[ts - 1:ts, :]


def _fox_gate(h, wf_pad, bf_pad, batch, seq, ts=256):
    m, d = h.shape
    spb = seq // ts
    return pl.pallas_call(
        functools.partial(_fox_gate_kernel, ts=ts),
        grid=(batch, spb),
        in_specs=[pl.BlockSpec((ts, d), lambda b, s: (b * spb + s, 0)),
                  pl.BlockSpec((d, LANES), lambda b, s: (0, 0)),
                  pl.BlockSpec((1, LANES), lambda b, s: (0, 0))],
        out_specs=pl.BlockSpec((ts, LANES), lambda b, s: (b * spb + s, 0)),
        out_shape=jax.ShapeDtypeStruct((m, LANES), F32),
        scratch_shapes=[pltpu.VMEM((1, LANES), F32)],
        compiler_params=_params("parallel", "arbitrary"),
        name="fox_gate",
    )(h, wf_pad, bf_pad)


def _fox_attn_kernel(q_ref, k_ref, v_ref, cq_ref, ck_ref, o_ref, *, tq, tk, scale):
    head = pl.program_id(1)
    i = pl.program_id(2)
    ratio = tq // tk
    hd = q_ref.shape[1]
    lane = lax.broadcasted_iota(jnp.int32, (tq, LANES), 1)
    cq = jnp.sum(jnp.where(lane == head, cq_ref[...], 0.0), axis=1, keepdims=True)
    row_id = lax.broadcasted_iota(jnp.int32, (tk, tk), 0)
    col_id = lax.broadcasted_iota(jnp.int32, (tk, tk), 1)

    def update(j, row0, masked, carry):
        start = pl.multiple_of(j * tk, tk)
        kj = k_ref[pl.ds(start, tk), :]
        vj = v_ref[pl.ds(start, tk), :]
        ckj = ck_ref[0, 0, j]
        out = []
        for r0, r1 in [(row0, tq)]:
            m_prev, l_prev, acc = carry
            s = _dot_nt(q_ref[r0:r1, :], kj) * scale + cq[r0:r1] - ckj
            if masked:
                top = jnp.where(row_id >= col_id, s[:tk], -jnp.inf)
                s = top if r1 - r0 == tk else jnp.concatenate([top, s[tk:]], axis=0)
            m_new = jnp.maximum(m_prev, jnp.max(s, axis=1, keepdims=True))
            alpha = jnp.exp(m_prev - m_new)
            p = jnp.exp(s - m_new)
            l_new = alpha * l_prev + jnp.sum(p, axis=1, keepdims=True)
            out.append((m_new, l_new, alpha * acc + _dot(p.astype(BF16), vj)))
        return tuple(jnp.concatenate(xs, axis=0) for xs in zip(*out))

    init = (jnp.full((tq, 1), -jnp.inf, F32), jnp.zeros((tq, 1), F32), jnp.zeros((tq, hd), F32))
    carry = lax.fori_loop(0, ratio * i, lambda j, c: update(j, 0, False, c), init)
    done = []
    for t in range(ratio):
        carry = update(ratio * i + t, t * tk, True, carry)
        done.append(tuple(x[:tk] for x in carry))
        if t + 1 < ratio:
            carry = tuple(x[tk:] for x in carry)
    l_fin = jnp.concatenate([d[1] for d in done], axis=0)
    acc = jnp.concatenate([d[2] for d in done], axis=0)
    o_ref[...] = (acc / l_fin).astype(o_ref.dtype)


def _fox_attention(qkv, cum, batch, seq, heads, tq=2048, tk=256):
    m = qkv.shape[0]
    hd = qkv.shape[1] // (3 * heads)
    nq = seq // tq
    nk = seq // tk
    ck = cum[:, :heads].reshape(batch, seq, heads).transpose(0, 2, 1).reshape(batch, heads, nk, 1, tk)
    return pl.pallas_call(
        functools.partial(_fox_attn_kernel, tq=tq, tk=tk, scale=hd ** -0.5),
        grid=(batch, heads, nq),
        in_specs=[pl.BlockSpec((tq, hd), lambda b, h, i: (b * nq + i, h)),
                  pl.BlockSpec((seq, hd), lambda b, h, i: (b, heads + h)),
                  pl.BlockSpec((seq, hd), lambda b, h, i: (b, 2 * heads + h)),
                  pl.BlockSpec((tq, LANES), lambda b, h, i: (b * nq + i, 0)),
                  pl.BlockSpec((1, 1, nk, 1, tk), lambda b, h, i: (b, h, 0, 0, 0))],
        out_specs=pl.BlockSpec((tq, hd), lambda b, h, i: (b * nq + i, h)),
        out_shape=jax.ShapeDtypeStruct((m, heads * hd), BF16),
        compiler_params=_params("parallel", "parallel", "arbitrary"),
        name="fox_attention",
    )(qkv, qkv, qkv, cum, ck)


def _ffn_in_kernel(h_ref, w1_ref, w2_ref, cw1_ref, cw2_ref, cb1_ref, cb2_ref, o_ref, tail_ref, wb_ref,
                   *, tm, sub, tpb):
    @pl.when(pl.program_id(1) % tpb == 0)
    def _():
        tail_ref[...] = jnp.zeros_like(tail_ref)

    @pl.when(pl.program_id(1) == 0)
    def _():
        wb_ref[0] = w1_ref[...].astype(BF16)
        wb_ref[1] = w2_ref[...].astype(BF16)

    row = lax.broadcasted_iota(jnp.int32, (sub, 1), 0)
    w_refs = (wb_ref.at[0], wb_ref.at[1])
    cw = (cw1_ref[...], cw2_ref[...])
    cb = (cb1_ref[...], cb2_ref[...])

    def up(r):
        rows = h_ref[r * sub:(r + 1) * sub, :]
        return [_dot(rows, w_ref[...]) for w_ref in w_refs]

    def conv(n, u, tail):
        prev1 = jnp.where(row == 0, tail[SUBLANES - 1:SUBLANES], pltpu.roll(u, 1, 0))
        prev2 = jnp.where(row == 0, tail[SUBLANES - 2:SUBLANES - 1],
                          jnp.where(row == 1, tail[SUBLANES - 1:SUBLANES], pltpu.roll(u, 2, 0)))
        return cw[n][0:1] * prev2 + cw[n][1:2] * prev1 + cw[n][2:3] * u + cb[n]

    nsub = tm // sub
    tails = [tail_ref[0], tail_ref[1]]
    u = up(0)
    for r in range(nsub):
        u_next = up(r + 1) if r + 1 < nsub else None
        x1 = conv(0, u[0], tails[0])
        x2 = conv(1, u[1], tails[1])
        gelu = 0.5 * x1 * (1.0 + lax.erf(x1 * (2.0 ** -0.5)))
        o_ref[r * sub:(r + 1) * sub, :] = (gelu * x2).astype(o_ref.dtype)
        tails = [x[sub - SUBLANES:] for x in u]
        u = u_next
    tail_ref[0] = tails[0]
    tail_ref[1] = tails[1]


def _ffn_in(h, w_in, conv_w, conv_b, layer, seq, tm=1024, tn=512, sub=256):
    m, d = h.shape
    f = w_in.shape[2] // 2
    nf = f // tn
    tpb = seq // tm
    return pl.pallas_call(
        functools.partial(_ffn_in_kernel, tm=tm, sub=sub, tpb=tpb),
        grid=(nf, m // tm),
        in_specs=[pl.BlockSpec((tm, d), lambda j, i: (i, 0)),
                  pl.BlockSpec((None, d, tn), lambda j, i: (layer, 0, j)),
                  pl.BlockSpec((None, d, tn), lambda j, i: (layer, 0, nf + j)),
                  pl.BlockSpec((None, CONV_WIDTH, tn), lambda j, i: (layer, 0, j)),
                  pl.BlockSpec((None, CONV_WIDTH, tn), lambda j, i: (layer, 0, nf + j)),
                  pl.BlockSpec((None, 1, tn), lambda j, i: (layer, 0, j)),
                  pl.BlockSpec((None, 1, tn), lambda j, i: (layer, 0, nf + j))],
        out_specs=pl.BlockSpec((tm, tn), lambda j, i: (i, j)),
        out_shape=jax.ShapeDtypeStruct((m, f), BF16),
        scratch_shapes=[pltpu.VMEM((2, SUBLANES, tn), F32), pltpu.VMEM((2, d, tn), BF16)],
        compiler_params=_params("parallel", "arbitrary"),
        name="ffn_in",
    )(h, w_in, w_in, conv_w, conv_w, conv_b, conv_b)


def _rwkv_pre_kernel(x_ref, halo_ref, g_ref, sc_ref, sh_ref, mu_ref, *o_refs, tm, tpb):
    first = pl.program_id(0) % tpb == 0
    g = g_ref[...]
    mod = lambda t: _rms(t, g) * (1.0 + sc_ref[0]) + sh_ref[0]
    h = mod(x_ref[...])
    h_halo = jnp.where(first, 0.0, mod(halo_ref[...]))
    row = lax.broadcasted_iota(jnp.int32, (tm, 1), 0)
    prev = jnp.where(row == 0, h_halo[SUBLANES - 1:SUBLANES], pltpu.roll(h, 1, 0))
    xx = prev - h
    mu = mu_ref[...]
    for n, o_ref in enumerate(o_refs):
        o_ref[...] = (h + xx * mu[n:n + 1]).astype(o_ref.dtype)


def _rwkv_pre(x2, g, sc, sh, mu, seq, tm=256):
    m, d = x2.shape
    tpb = seq // tm
    hb = tm // SUBLANES
    nmix = mu.shape[0]
    return pl.pallas_call(
        functools.partial(_rwkv_pre_kernel, tm=tm, tpb=tpb),
        grid=(m // tm,),
        in_specs=[pl.BlockSpec((tm, d), lambda i: (i, 0)),
                  pl.BlockSpec((SUBLANES, d), lambda i: (jnp.maximum(i * hb - 1, 0), 0)),
                  pl.BlockSpec((1, d), lambda i: (0, 0)),
                  pl.BlockSpec((1, 1, d), lambda i: (i // tpb, 0, 0)),
                  pl.BlockSpec((1, 1, d), lambda i: (i // tpb, 0, 0)),
                  pl.BlockSpec((nmix, d), lambda i: (0, 0))],
        out_specs=[pl.BlockSpec((tm, d), lambda i: (i, 0))] * nmix,
        out_shape=[jax.ShapeDtypeStruct((m, d), BF16)] * nmix,
        compiler_params=_params("parallel"),
        name="rwkv_pre",
    )(x2, x2, g.reshape(1, d), sc, sh, mu)


def _lora_kernel(x_ref, w1_ref, w2_ref, o_ref, *, act):
    t = _dot(x_ref[...], w1_ref[...])
    if act == "tanh":
        t = jnp.tanh(t)
    elif act == "sigmoid":
        t = jax.nn.sigmoid(t)
    o_ref[...] = _dot(t.astype(BF16), w2_ref[...])


def _lora(x, w1, w2, act, tm=512):
    m, d = x.shape
    r = w1.shape[1]
    n = w2.shape[1]
    return pl.pallas_call(
        functools.partial(_lora_kernel, act=act),
        grid=(m // tm,),
        in_specs=[pl.BlockSpec((tm, d), lambda i: (i, 0)),
                  pl.BlockSpec((d, r), lambda i: (0, 0)),
                  pl.BlockSpec((r, n), lambda i: (0, 0))],
        out_specs=pl.BlockSpec((tm, n), lambda i: (i, 0)),
        out_shape=jax.ShapeDtypeStruct((m, n), F32),
        compiler_params=_params("parallel"),
        name="rwkv_lora_" + act,
    )(x, w1, w2)


def _wkv_kernel(r_ref, k_ref, v_ref, wl_ref, al_ref, g_ref,
                w0_ref, a0_ref, kk_ref, ka_ref, rk_ref, gnw_ref, gnb_ref,
                o_ref, s_ref, *, groups, chunks):
    L = WKV_CHUNK
    N = RWKV_HEAD_DIM
    W = MXU_DIM

    @pl.when(pl.program_id(2) == 0)
    def _():
        s_ref[...] = jnp.zeros_like(s_ref)

    ri = lax.broadcasted_iota(jnp.int32, (W, W), 0)
    ci = lax.broadcasted_iota(jnp.int32, (W, W), 1)
    bd = (ri // N) == (ci // N)
    bd_bf = jnp.where(bd, 1.0, 0.0).astype(BF16)
    t_i = lax.broadcasted_iota(jnp.int32, (L, W), 0)
    s_i = lax.broadcasted_iota(jnp.int32, (L, W), 1) % N
    strict = t_i > s_i
    incl = t_i >= s_i
    eye = jnp.where(t_i == s_i, 1.0, 0.0)
    tri = jnp.where(lax.broadcasted_iota(jnp.int32, (L, L), 0) >= lax.broadcasted_iota(jnp.int32, (L, L), 1),
                    1.0, 0.0).astype(BF16)

    def block_diag(x):
        xb = x.astype(BF16)
        return jnp.concatenate([xb] * (W // L), axis=0) * bd_bf

    def stack(a, b):
        return jnp.concatenate([a, b], axis=0).astype(BF16)

    def head_sums(xs):
        res = _dot(jnp.concatenate(xs, axis=0).astype(BF16), bd_bf)
        return [res[n * L:(n + 1) * L] for n in range(len(xs))]

    def each(fn, *cols):
        return [fn(*args) for args in zip(*cols)]

    def cumsum(x):
        hi, mid, lo = _split3(x)
        return _dot(tri, hi) + _dot(tri, mid) + _dot(tri, lo)

    lanes = [slice(gi * W, (gi + 1) * W) for gi in range(groups)]

    def par(ref):
        return [ref[:, sl] for sl in lanes]

    def prepare(c):
        rows = slice(c * L, (c + 1) * L)
        tok = lambda ref: [ref[rows, sl] for sl in lanes]
        r, v, k_raw = tok(r_ref), tok(v_ref), tok(k_ref)
        lw = each(lambda w0, wl: -math.exp(-0.5) * jax.nn.sigmoid(w0 + wl), par(w0_ref), tok(wl_ref))
        a = each(lambda a0, al: jax.nn.sigmoid(a0 + al), par(a0_ref), tok(al_ref))
        kk = each(lambda kx, kkp: kx * kkp, k_raw, par(kk_ref))
        k = each(lambda kx, ax, ka: kx * (1.0 + (ax - 1.0) * ka), k_raw, a, par(ka_ref))
        kk = each(lambda x, sq: x / jnp.maximum(jnp.sqrt(sq), 1e-12), kk, head_sums([x * x for x in kk]))
        cum = each(cumsum, lw)
        cum_last = [cx[L - 1:L] for cx in cum]
        p_inv = each(lambda cx: jnp.exp(-cx), cum)
        p_rest = each(lambda cx, cl: jnp.exp(cl - cx), cum, cum_last)
        b = each(lambda x, ax: x * ax, kk, a)
        lhs_ar = each(lambda x, cx, lx, rx: stack(-x * jnp.exp(cx - lx), rx * jnp.exp(cx)), kk, cum, lw, r)
        rkr = each(lambda rx, kx, rk: rx * kx * rk, r, k, par(rk_ref))
        bonus = each(lambda sx, vx: sx * vx, head_sums(rkr), v)
        decay = [jnp.exp(cl) for cl in cum_last]
        kb_rest = each(lambda bx, kx, pr: stack(bx * pr, kx * pr), b, k, p_rest)
        b_w = each(lambda bx, pi: block_diag(bx * pi), b, p_inv)
        k_w = each(lambda kx, pi: block_diag(kx * pi), k, p_inv)
        ab = each(_dot_nt, lhs_ar, b_w)
        a_ab = [jnp.where(strict, x[:L], 0.0) for x in ab]
        a_rb = [jnp.where(incl, x[L:], 0.0) for x in ab]
        inv = [eye + x for x in a_ab]
        npow = [x.astype(BF16) for x in a_ab]
        npow = each(lambda n: _dot(n, block_diag(n)).astype(BF16), npow)
        for _ in range(4):
            res = each(lambda n, t: _dot(jnp.concatenate([n, t.astype(BF16)], axis=0), block_diag(n)), npow, inv)
            npow = [x[:L].astype(BF16) for x in res]
            inv = [t + x[L:] for t, x in zip(inv, res)]
        inv = each(lambda t, n: t + _dot(t.astype(BF16), block_diag(n)), inv, npow)
        ak = each(_dot_nt, lhs_ar, k_w)
        a_ak = [jnp.where(strict, x[:L], 0.0) for x in ak]
        a_rk = [jnp.where(incl, x[L:], 0.0) for x in ak]
        a_v = each(lambda x, y, vx: _dot(stack(x, y), block_diag(vx)), a_ak, a_rk, v)
        return dict(rows=rows, v=v, lhs_ar=lhs_ar, inv=inv, a_v=a_v, a_rb=a_rb, bonus=bonus,
                    decay=decay, kb_rest=kb_rest, gate=tok(g_ref))

    prepared = [prepare(c) for c in range(chunks)]
    gn_w, gn_b = par(gnw_ref), par(gnb_ref)

    state = [s_ref[gi] for gi in range(groups)]
    for q in prepared:
        a_s = each(lambda lhs, s: _dot_nt(lhs, s.astype(BF16)), q["lhs_ar"], state)
        z = each(lambda t, av, as_: _dot(t.astype(BF16), block_diag(av[:L] + as_[:L])), q["inv"], q["a_v"], a_s)
        y = each(lambda as_, av, arb, zx: as_[L:] + av[L:] + _dot(arb.astype(BF16), block_diag(zx)),
                 a_s, q["a_v"], q["a_rb"], z)
        upd = each(lambda zx, vx, kb: _dot_tn(stack(zx, vx), kb), z, q["v"], q["kb_rest"])
        state = each(lambda s, dx, ux: jnp.where(bd, s * dx + ux, 0.0), state, q["decay"], upd)

        mean = [m * (1.0 / N) for m in head_sums(y)]
        yc = each(lambda yx, mx: yx - mx, y, mean)
        var = [m * (1.0 / N) for m in head_sums([x * x for x in yc])]
        for gi, sl in enumerate(lanes):
            yn = yc[gi] * lax.rsqrt(var[gi] + GN_EPS) * gn_w[gi] + gn_b[gi]
            o_ref[q["rows"], sl] = ((yn + q["bonus"][gi]) * q["gate"][gi]).astype(o_ref.dtype)
    for gi in range(groups):
        s_ref[gi] = state[gi]


def _wkv(r, k, v, wl, al, g, w0, a0, k_k, k_a, r_k, gn_w, gn_b, batch, seq, groups=8, chunks=2):
    m, d = r.shape
    rows = chunks * WKV_CHUNK
    gw = groups * MXU_DIM
    ng = d // gw
    nc = seq // rows
    tok = pl.BlockSpec((rows, gw), lambda b, gidx, c: (b * nc + c, gidx))
    par = pl.BlockSpec((1, gw), lambda b, gidx, c: (0, gidx))
    row = lambda t: t.reshape(1, d)
    return pl.pallas_call(
        functools.partial(_wkv_kernel, groups=groups, chunks=chunks),
        grid=(batch, ng, nc),
        in_specs=[tok] * 6 + [par] * 7,
        out_specs=tok,
        out_shape=jax.ShapeDtypeStruct((m, d), BF16),
        scratch_shapes=[pltpu.VMEM((groups, MXU_DIM, MXU_DIM), F32)],
        compiler_params=_params("parallel", "parallel", "arbitrary"),
        name="wkv7",
    )(r, k, v, wl, al, g, row(w0), row(a0), row(k_k), row(k_a), row(r_k), row(gn_w), row(gn_b))


def _pad_to(x, axis, size):
    pad = [(0, 0)] * x.ndim
    pad[axis] = (0, size - x.shape[axis])
    return jnp.pad(x, pad)


def _round_up(n, mult):
    return -(-n // mult) * mult


def kernel(x, c, mod_w, mod_b, norm_g, fox_w_in, fox_b_f, fox_w_out, rwkv_mu, rwkv_w0, rwkv_w1, rwkv_w2, rwkv_a0, rwkv_a1, rwkv_a2, rwkv_g1, rwkv_g2, rwkv_k_k, rwkv_k_a, rwkv_r_k, rwkv_w_r, rwkv_w_k, rwkv_w_v, rwkv_w_o, rwkv_gn_w, rwkv_gn_b, ffn_w_in, ffn_conv_w, ffn_conv_b, ffn_w_out):
    batch, seq, d = x.shape
    depth = mod_w.shape[0]
    heads = fox_b_f.shape[1]
    bf = lambda t: t.astype(BF16)

    mod = _modulation(c, mod_w, mod_b)
    x2 = x.reshape(batch * seq, d)

    h = None
    for i in range(depth):
        sh_m, sc_m, g_m, sh_f, sc_f, g_f = [mod[i, :, n] for n in range(6)]
        j = i // 2
        ffn_pre = (norm_g[i, 2], sc_f, sh_f)
        if i % 2 == 0:
            if h is None:
                h = _prenorm(x2, norm_g[i, 0], sc_m, sh_m, seq)
            qkv = _matmul(h, fox_w_in, j, 3 * d, BF16, name="fox_qkv")
            wf = _pad_to(bf(fox_w_in[j, :, 3 * d:]), 1, LANES)
            bfp = _pad_to(fox_b_f[j].reshape(1, heads), 1, LANES)
            cum = _fox_gate(h, wf, bfp, batch, seq)
            o = _fox_attention(qkv, cum, batch, seq, heads)
            x2, h = _matmul_postnorm(o, bf(fox_w_out[j]), x2, norm_g[i, 1], g_m, seq, nxt=ffn_pre)
        else:
            xr, xw, xk, xv, xa, xg = _rwkv_pre(x2, norm_g[i, 0], sc_m, sh_m, rwkv_mu[j], seq)
            r = _matmul(xr, rwkv_w_r, j, d, F32, name="rwkv_r")
            k = _matmul(xk, rwkv_w_k, j, d, F32, name="rwkv_k")
            v = _matmul(xv, rwkv_w_v, j, d, F32, name="rwkv_v")
            lr = _round_up(rwkv_w1.shape[2], LANES)
            wl = _lora(xw, _pad_to(bf(rwkv_w1[j]), 1, lr), _pad_to(bf(rwkv_w2[j]), 0, lr), "tanh")
            al = _lora(xa, _pad_to(bf(rwkv_a1[j]), 1, lr), _pad_to(bf(rwkv_a2[j]), 0, lr), "none")
            gr = _round_up(rwkv_g1.shape[2], LANES)
            g = _lora(xg, _pad_to(bf(rwkv_g1[j]), 1, gr), _pad_to(bf(rwkv_g2[j]), 0, gr), "sigmoid")
            o = _wkv(r, k, v, wl, al, g, rwkv_w0[j], rwkv_a0[j], rwkv_k_k[j], rwkv_k_a[j],
                     rwkv_r_k[j], rwkv_gn_w[j], rwkv_gn_b[j], batch, seq)
            x2, h = _matmul_postnorm(o, bf(rwkv_w_o[j]), x2, norm_g[i, 1], g_m, seq, nxt=ffn_pre)
        act = _ffn_in(h, ffn_w_in, ffn_conv_w, ffn_conv_b.reshape(depth, 1, -1), i, seq)
        f = act.shape[1]
        nxt = None
        if i + 1 < depth and (i + 1) % 2 == 0:
            nxt = (norm_g[i + 1, 0], mod[i + 1, :, 1], mod[i + 1, :, 0])
        x2, h = _matmul_postnorm(act, bf(ffn_w_out[i]), x2, norm_g[i, 3], g_f, seq, nxt=nxt, tk=f // 2)
    return x2.reshape(batch, seq, d)
```

```python
import functools
import math

import jax
import jax.numpy as jnp
from jax import lax
from jax.experimental import pallas as pl
from jax.experimental.pallas import tpu as pltpu

F32 = jnp.float32
BF16 = jnp.bfloat16

LOG2E = math.log2(math.e)
NORM_EPS = 1e-6
GN_EPS = 64e-5
FOX_HEADS = 16
RWKV_HEAD_DIM = 64
CONV_WIDTH = 3

LANES = 128
SUBLANES = 8
MXU_DIM = 256
WKV_CHUNK = 64
WKV_HEADS_PER_GROUP = MXU_DIM // RWKV_HEAD_DIM
VMEM_LIMIT = 56 << 20


def _params(*sem):
    return pltpu.CompilerParams(dimension_semantics=sem, vmem_limit_bytes=VMEM_LIMIT)


def _rms(x, g):
    return x * lax.rsqrt(jnp.mean(x * x, axis=-1, keepdims=True) + NORM_EPS) * g


def _dot(a, b):
    return jnp.dot(a, b, preferred_element_type=F32)


def _dot_nt(a, b):
    return lax.dot_general(a, b, (((1,), (1,)), ((), ())), preferred_element_type=F32)


def _dot_tn(a, b):
    return lax.dot_general(a, b, (((0,), (0,)), ((), ())), preferred_element_type=F32)


def _split2(x):
    hi = x.astype(BF16)
    lo = (x - hi.astype(F32)).astype(BF16)
    return hi, lo


def _split3(x):
    hi = x.astype(BF16)
    r1 = x - hi.astype(F32)
    mid = r1.astype(BF16)
    lo = (r1 - mid.astype(F32)).astype(BF16)
    return hi, mid, lo


def _softplus(x):
    return jnp.maximum(x, 0.0) + jnp.log1p(jnp.exp(-jnp.abs(x)))


def _mod_kernel(c_ref, w_ref, b_ref, o_ref):
    c = c_ref[...]
    s = c * jax.nn.sigmoid(c)
    o_ref[0] = _dot(s.astype(BF16), w_ref[0].astype(BF16)) + b_ref[0]


def _modulation(c, mod_w, mod_b):
    depth, d, n = mod_w.shape
    b = c.shape[0]
    rows = -(-b // SUBLANES) * SUBLANES
    c_pad = jnp.pad(c, ((0, rows - b), (0, 0)))
    tn = 1024
    out = pl.pallas_call(
        _mod_kernel,
        grid=(depth, n // tn),
        in_specs=[pl.BlockSpec((rows, d), lambda l, j: (0, 0)),
                  pl.BlockSpec((1, d, tn), lambda l, j: (l, 0, j)),
                  pl.BlockSpec((1, 1, tn), lambda l, j: (l, 0, j))],
        out_specs=pl.BlockSpec((1, rows, tn), lambda l, j: (l, 0, j)),
        out_shape=jax.ShapeDtypeStruct((depth, rows, n), F32),
        compiler_params=_params("parallel", "parallel"),
        name="adaln_mod",
    )(c_pad, mod_w, mod_b.reshape(depth, 1, n))
    return out[:, :b].reshape(depth, b, 6, 1, d)


def _prenorm_kernel(x_ref, g_ref, sc_ref, sh_ref, o_ref):
    h = _rms(x_ref[...], g_ref[...]) * (1.0 + sc_ref[0]) + sh_ref[0]
    o_ref[...] = h.astype(o_ref.dtype)


def _prenorm(x2, g, sc, sh, seq, tm=512):
    m, d = x2.shape
    tpb = seq // tm
    return pl.pallas_call(
        _prenorm_kernel,
        grid=(m // tm,),
        in_specs=[pl.BlockSpec((tm, d), lambda i: (i, 0)),
                  pl.BlockSpec((1, d), lambda i: (0, 0)),
                  pl.BlockSpec((1, 1, d), lambda i: (i // tpb, 0, 0)),
                  pl.BlockSpec((1, 1, d), lambda i: (i // tpb, 0, 0))],
        out_specs=pl.BlockSpec((tm, d), lambda i: (i, 0)),
        out_shape=jax.ShapeDtypeStruct((m, d), BF16),
        compiler_params=_params("parallel"),
        name="prenorm",
    )(x2, g.reshape(1, d), sc, sh)


def _mm_kernel(a_ref, w_ref, o_ref, wb_ref):
    @pl.when(pl.program_id(1) == 0)
    def _():
        wb_ref[...] = w_ref[...].astype(BF16)

    o_ref[...] = _dot(a_ref[...], wb_ref[...]).astype(o_ref.dtype)


def _matmul(a, w, layer, n, out_dtype, tm=2048, tn=512, name="matmul"):
    m, k = a.shape
    return pl.pallas_call(
        _mm_kernel,
        grid=(n // tn, m // tm),
        in_specs=[pl.BlockSpec((tm, k), lambda j, i: (i, 0)),
                  pl.BlockSpec((None, k, tn), lambda j, i: (layer, 0, j))],
        out_specs=pl.BlockSpec((tm, tn), lambda j, i: (i, j)),
        out_shape=jax.ShapeDtypeStruct((m, n), out_dtype),
        scratch_shapes=[pltpu.VMEM((k, tn), BF16)],
        compiler_params=_params("parallel", "arbitrary"),
        name=name,
    )(a, w)


def _out_kernel(*refs, nk, with_h, cast_w):
    a_ref, w_ref, x_ref, g1_ref, gate_ref = refs[:5]
    if with_h:
        g2_ref, sc_ref, sh_ref, xo_ref, ho_ref = refs[5:10]
    else:
        xo_ref = refs[5]
    acc_ref = refs[-1] if nk > 1 else None
    if cast_w:
        wb_ref = refs[-1]

        @pl.when(pl.program_id(0) == 0)
        def _():
            wb_ref[...] = w_ref[...].astype(BF16)

        w_ref = wb_ref

    def epilogue(y):
        xn = x_ref[...] + gate_ref[0] * _rms(y, g1_ref[...])
        xo_ref[...] = xn
        if with_h:
            h = _rms(xn, g2_ref[...]) * (1.0 + sc_ref[0]) + sh_ref[0]
            ho_ref[...] = h.astype(ho_ref.dtype)

    part = _dot(a_ref[...], w_ref[...])
    if nk == 1:
        epilogue(part)
        return
    k = pl.program_id(1)

    @pl.when(k == 0)
    def _():
        acc_ref[...] = part

    @pl.when(jnp.logical_and(k > 0, k < nk - 1))
    def _():
        acc_ref[...] += part

    @pl.when(k == nk - 1)
    def _():
        epilogue(acc_ref[...] + part)


def _matmul_postnorm(a, w, x2, g1, gate, seq, nxt=None, tm=512, tk=None, layer=None):
    m, kdim = a.shape
    d = w.shape[-1]
    cast_w = layer is not None
    tk = kdim if tk is None else tk
    nk = kdim // tk
    assert not (cast_w and nk > 1)
    tpb = seq // tm
    with_h = nxt is not None
    row = lambda i, k: (i, 0)
    per_batch = lambda i, k: (i // tpb, 0, 0)
    const = lambda i, k: (0, 0)
    if cast_w:
        w_spec = pl.BlockSpec((None, kdim, d), lambda i, k: (layer, 0, 0), pipeline_mode=pl.Buffered(1))
    else:
        w_spec = pl.BlockSpec((tk, d), lambda i, k: (k, 0))
    in_specs = [pl.BlockSpec((tm, tk), lambda i, k: (i, k)),
                w_spec,
                pl.BlockSpec((tm, d), row),
                pl.BlockSpec((1, d), const),
                pl.BlockSpec((1, 1, d), per_batch)]
    args = [a, w, x2, g1.reshape(1, d), gate]
    out_specs = [pl.BlockSpec((tm, d), row)]
    out_shape = [jax.ShapeDtypeStruct((m, d), F32)]
    if with_h:
        g2, sc, sh = nxt
        in_specs += [pl.BlockSpec((1, d), const), pl.BlockSpec((1, 1, d), per_batch),
                     pl.BlockSpec((1, 1, d), per_batch)]
        args += [g2.reshape(1, d), sc, sh]
        out_specs.append(pl.BlockSpec((tm, d), row))
        out_shape.append(jax.ShapeDtypeStruct((m, d), BF16))
    res = pl.pallas_call(
        functools.partial(_out_kernel, nk=nk, with_h=with_h, cast_w=cast_w),
        grid=(m // tm, nk),
        in_specs=in_specs,
        out_specs=out_specs,
        out_shape=out_shape,
        scratch_shapes=([pltpu.VMEM((tm, d), F32)] if nk > 1 else [])
        + ([pltpu.VMEM((kdim, d), BF16)] if cast_w else []),
        compiler_params=_params("arbitrary", "arbitrary"),
        name="matmul_postnorm",
    )(*args)
    return (res[0], res[1]) if with_h else (res[0], None)


def _fox_gate_kernel(h_ref, wf_ref, bf_ref, o_ref, carry_ref, *, ts):
    @pl.when(pl.program_id(1) == 0)
    def _():
        carry_ref[...] = jnp.zeros_like(carry_ref)

    logit = _dot(h_ref[...], wf_ref[...]) + bf_ref[...]
    log_f = -_softplus(-logit)
    row = lax.broadcasted_iota(jnp.int32, (ts, ts), 0)
    col = lax.broadcasted_iota(jnp.int32, (ts, ts), 1)
    tri = jnp.where(row >= col, 1.0, 0.0).astype(BF16)
    hi, mid, lo = _split3(log_f)
    cs = _dot(tri, hi) + _dot(tri, mid) + _dot(tri, lo) + carry_ref[...]
    o_ref[...] = cs
    carry_ref[...] = cs[ts - 1:ts, :]


def _fox_gate(h, wf_pad, bf_pad, batch, seq, ts=256):
    m, d = h.shape
    spb = seq // ts
    return pl.pallas_call(
        functools.partial(_fox_gate_kernel, ts=ts),
        grid=(batch, spb),
        in_specs=[pl.BlockSpec((ts, d), lambda b, s: (b * spb + s, 0)),
                  pl.BlockSpec((d, LANES), lambda b, s: (0, 0)),
                  pl.BlockSpec((1, LANES), lambda b, s: (0, 0))],
        out_specs=pl.BlockSpec((ts, LANES), lambda b, s: (b * spb + s, 0)),
        out_shape=jax.ShapeDtypeStruct((m, LANES), F32),
        scratch_shapes=[pltpu.VMEM((1, LANES), F32)],
        compiler_params=_params("parallel", "arbitrary"),
        name="fox_gate",
    )(h, wf_pad, bf_pad)


def _fox_attn_kernel(q_ref, k_ref, v_ref, cq_ref, ck_ref, o_ref, *, tq, tk, scale):
    head = pl.program_id(1)
    i = pl.program_id(2)
    ratio = tq // tk
    hd = q_ref.shape[1]
    lane = lax.broadcasted_iota(jnp.int32, (tq, LANES), 1)
    cq = jnp.sum(jnp.where(lane == head, cq_ref[...], 0.0), axis=1, keepdims=True) * LOG2E
    row_id = lax.broadcasted_iota(jnp.int32, (tk, tk), 0)
    col_id = lax.broadcasted_iota(jnp.int32, (tk, tk), 1)

    def update(j, row0, masked, carry):
        start = pl.multiple_of(j * tk, tk)
        kj = k_ref[pl.ds(start, tk), :]
        vj = v_ref[pl.ds(start, tk), :]
        ckj = ck_ref[0, 0, j] * LOG2E
        m_prev, l_prev, acc = carry
        s = _dot_nt(q_ref[row0:, :], kj) * (scale * LOG2E) + cq[row0:] - ckj
        if masked:
            top = jnp.where(row_id >= col_id, s[:tk], -jnp.inf)
            s = top if tq - row0 == tk else jnp.concatenate([top, s[tk:]], axis=0)
        m_new = jnp.maximum(m_prev, jnp.max(s, axis=1, keepdims=True))
        alpha = jnp.exp2(m_prev - m_new)
        p = jnp.exp2(s - m_new)
        l_new = alpha * l_prev + jnp.sum(p, axis=1, keepdims=True)
        return m_new, l_new, alpha * acc + _dot(p.astype(BF16), vj)

    init = (jnp.full((tq, 1), -jnp.inf, F32), jnp.zeros((tq, 1), F32), jnp.zeros((tq, hd), F32))
    carry = lax.fori_loop(0, ratio * i, lambda j, c: update(j, 0, False, c), init)
    done = []
    for t in range(ratio):
        carry = update(ratio * i + t, t * tk, True, carry)
        done.append(tuple(x[:tk] for x in carry))
        if t + 1 < ratio:
            carry = tuple(x[tk:] for x in carry)
    l_fin = jnp.concatenate([d[1] for d in done], axis=0)
    acc = jnp.concatenate([d[2] for d in done], axis=0)
    o_ref[...] = (acc / l_fin).astype(o_ref.dtype)


def _fox_attention(qkv, cum, batch, seq, heads, tq=2048, tk=256):
    m = qkv.shape[0]
    hd = qkv.shape[1] // (3 * heads)
    nq = seq // tq
    nk = seq // tk
    ck = cum[:, :heads].reshape(batch, seq, heads).transpose(0, 2, 1).reshape(batch, heads, nk, 1, tk)
    return pl.pallas_call(
        functools.partial(_fox_attn_kernel, tq=tq, tk=tk, scale=hd ** -0.5),
        grid=(batch, heads, nq),
        in_specs=[pl.BlockSpec((tq, hd), lambda b, h, i: (b * nq + i, h)),
                  pl.BlockSpec((seq, hd), lambda b, h, i: (b, heads + h)),
                  pl.BlockSpec((seq, hd), lambda b, h, i: (b, 2 * heads + h)),
                  pl.BlockSpec((tq, LANES), lambda b, h, i: (b * nq + i, 0)),
                  pl.BlockSpec((1, 1, nk, 1, tk), lambda b, h, i: (b, h, 0, 0, 0))],
        out_specs=pl.BlockSpec((tq, hd), lambda b, h, i: (b * nq + i, h)),
        out_shape=jax.ShapeDtypeStruct((m, heads * hd), BF16),
        compiler_params=_params("parallel", "parallel", "arbitrary"),
        name="fox_attention",
    )(qkv, qkv, qkv, cum, ck)


def _ffn_in_kernel(h_ref, w1_ref, w2_ref, cw1_ref, cw2_ref, cb1_ref, cb2_ref, o_ref, tail_ref, wb_ref,
                   *, tm, sub, tpb):
    @pl.when(pl.program_id(1) % tpb == 0)
    def _():
        tail_ref[...] = jnp.zeros_like(tail_ref)

    @pl.when(pl.program_id(1) == 0)
    def _():
        wb_ref[0] = w1_ref[...].astype(BF16)
        wb_ref[1] = w2_ref[...].astype(BF16)

    row = lax.broadcasted_iota(jnp.int32, (sub, 1), 0)
    w_refs = (wb_ref.at[0], wb_ref.at[1])
    cw = (cw1_ref[...], cw2_ref[...])
    cb = (cb1_ref[...], cb2_ref[...])

    def up(r):
        rows = h_ref[r * sub:(r + 1) * sub, :]
        return [_dot(rows, w_ref[...]) for w_ref in w_refs]

    def conv(n, u, tail):
        prev1 = jnp.where(row == 0, tail[SUBLANES - 1:SUBLANES], pltpu.roll(u, 1, 0))
        prev2 = jnp.where(row == 0, tail[SUBLANES - 2:SUBLANES - 1],
                          jnp.where(row == 1, tail[SUBLANES - 1:SUBLANES], pltpu.roll(u, 2, 0)))
        return cw[n][0:1] * prev2 + cw[n][1:2] * prev1 + cw[n][2:3] * u + cb[n]

    nsub = tm // sub
    tails = [tail_ref[0], tail_ref[1]]
    u = up(0)
    for r in range(nsub):
        u_next = up(r + 1) if r + 1 < nsub else None
        x1 = conv(0, u[0], tails[0])
        x2 = conv(1, u[1], tails[1])
        gelu = 0.5 * x1 * (1.0 + lax.erf(x1 * (2.0 ** -0.5)))
        o_ref[r * sub:(r + 1) * sub, :] = (gelu * x2).astype(o_ref.dtype)
        tails = [x[sub - SUBLANES:] for x in u]
        u = u_next
    tail_ref[0] = tails[0]
    tail_ref[1] = tails[1]


def _ffn_in(h, w_in, conv_w, conv_b, layer, seq, tm=2048, tn=512, sub=256):
    m, d = h.shape
    f = w_in.shape[2] // 2
    nf = f // tn
    tpb = seq // tm
    return pl.pallas_call(
        functools.partial(_ffn_in_kernel, tm=tm, sub=sub, tpb=tpb),
        grid=(nf, m // tm),
        in_specs=[pl.BlockSpec((tm, d), lambda j, i: (i, 0)),
                  pl.BlockSpec((None, d, tn), lambda j, i: (layer, 0, j)),
                  pl.BlockSpec((None, d, tn), lambda j, i: (layer, 0, nf + j)),
                  pl.BlockSpec((None, CONV_WIDTH, tn), lambda j, i: (layer, 0, j)),
                  pl.BlockSpec((None, CONV_WIDTH, tn), lambda j, i: (layer, 0, nf + j)),
                  pl.BlockSpec((None, 1, tn), lambda j, i: (layer, 0, j)),
                  pl.BlockSpec((None, 1, tn), lambda j, i: (layer, 0, nf + j))],
        out_specs=pl.BlockSpec((tm, tn), lambda j, i: (i, j)),
        out_shape=jax.ShapeDtypeStruct((m, f), BF16),
        scratch_shapes=[pltpu.VMEM((2, SUBLANES, tn), F32), pltpu.VMEM((2, d, tn), BF16)],
        compiler_params=_params("parallel", "arbitrary"),
        name="ffn_in",
    )(h, w_in, w_in, conv_w, conv_w, conv_b, conv_b)


def _rwkv_pre_kernel(x_ref, halo_ref, g_ref, sc_ref, sh_ref, mu_ref, *o_refs, tm, tpb):
    first = pl.program_id(0) % tpb == 0
    g = g_ref[...]
    mod = lambda t: _rms(t, g) * (1.0 + sc_ref[0]) + sh_ref[0]
    h = mod(x_ref[...])
    h_halo = jnp.where(first, 0.0, mod(halo_ref[...]))
    row = lax.broadcasted_iota(jnp.int32, (tm, 1), 0)
    prev = jnp.where(row == 0, h_halo[SUBLANES - 1:SUBLANES], pltpu.roll(h, 1, 0))
    xx = prev - h
    mu = mu_ref[...]
    for n, o_ref in enumerate(o_refs):
        o_ref[...] = (h + xx * mu[n:n + 1]).astype(o_ref.dtype)


def _rwkv_pre(x2, g, sc, sh, mu, seq, tm=256):
    m, d = x2.shape
    tpb = seq // tm
    hb = tm // SUBLANES
    nmix = mu.shape[0]
    return pl.pallas_call(
        functools.partial(_rwkv_pre_kernel, tm=tm, tpb=tpb),
        grid=(m // tm,),
        in_specs=[pl.BlockSpec((tm, d), lambda i: (i, 0)),
                  pl.BlockSpec((SUBLANES, d), lambda i: (jnp.maximum(i * hb - 1, 0), 0)),
                  pl.BlockSpec((1, d), lambda i: (0, 0)),
                  pl.BlockSpec((1, 1, d), lambda i: (i // tpb, 0, 0)),
                  pl.BlockSpec((1, 1, d), lambda i: (i // tpb, 0, 0)),
                  pl.BlockSpec((nmix, d), lambda i: (0, 0))],
        out_specs=[pl.BlockSpec((tm, d), lambda i: (i, 0))] * nmix,
        out_shape=[jax.ShapeDtypeStruct((m, d), BF16)] * nmix,
        compiler_params=_params("parallel"),
        name="rwkv_pre",
    )(x2, x2, g.reshape(1, d), sc, sh, mu)


def _lora_kernel(x_ref, w1_ref, w2_ref, o_ref, *, act):
    t = _dot(x_ref[...], w1_ref[...])
    if act == "tanh":
        t = jnp.tanh(t)
    elif act == "sigmoid":
        t = jax.nn.sigmoid(t)
    o_ref[...] = _dot(t.astype(BF16), w2_ref[...])


def _lora(x, w1, w2, act, tm=512):
    m, d = x.shape
    r = w1.shape[1]
    n = w2.shape[1]
    return pl.pallas_call(
        functools.partial(_lora_kernel, act=act),
        grid=(m // tm,),
        in_specs=[pl.BlockSpec((tm, d), lambda i: (i, 0)),
                  pl.BlockSpec((d, r), lambda i: (0, 0)),
                  pl.BlockSpec((r, n), lambda i: (0, 0))],
        out_specs=pl.BlockSpec((tm, n), lambda i: (i, 0)),
        out_shape=jax.ShapeDtypeStruct((m, n), F32),
        compiler_params=_params("parallel"),
        name="rwkv_lora_" + act,
    )(x, w1, w2)


def _wkv_kernel(r_ref, k_ref, v_ref, wl_ref, al_ref, g_ref,
                w0_ref, a0_ref, kk_ref, ka_ref, rk_ref, gnw_ref, gnb_ref,
                o_ref, s_ref, *, groups, chunks):
    L = WKV_CHUNK
    N = RWKV_HEAD_DIM
    W = MXU_DIM

    @pl.when(pl.program_id(2) == 0)
    def _():
        s_ref[...] = jnp.zeros_like(s_ref)

    ri = lax.broadcasted_iota(jnp.int32, (W, W), 0)
    ci = lax.broadcasted_iota(jnp.int32, (W, W), 1)
    bd = (ri // N) == (ci // N)
    bd_bf = jnp.where(bd, 1.0, 0.0).astype(BF16)
    t_i = lax.broadcasted_iota(jnp.int32, (L, W), 0)
    s_i = lax.broadcasted_iota(jnp.int32, (L, W), 1) % N
    strict = t_i > s_i
    incl = t_i >= s_i
    eye = jnp.where(t_i == s_i, 1.0, 0.0)
    tri = jnp.where(lax.broadcasted_iota(jnp.int32, (L, L), 0) >= lax.broadcasted_iota(jnp.int32, (L, L), 1),
                    1.0, 0.0).astype(BF16)

    def block_diag(x):
        xb = x.astype(BF16)
        return jnp.concatenate([xb] * (W // L), axis=0) * bd_bf

    def stack(a, b):
        return jnp.concatenate([a, b], axis=0).astype(BF16)

    def head_sums(xs):
        res = _dot(jnp.concatenate(xs, axis=0).astype(BF16), bd_bf)
        return [res[n * L:(n + 1) * L] for n in range(len(xs))]

    def each(fn, *cols):
        return [fn(*args) for args in zip(*cols)]

    def cumsum(x):
        hi, mid, lo = _split3(x)
        return _dot(tri, hi) + _dot(tri, mid) + _dot(tri, lo)

    lanes = [slice(gi * W, (gi + 1) * W) for gi in range(groups)]

    def par(ref):
        return [ref[:, sl] for sl in lanes]

    def prepare(c):
        rows = slice(c * L, (c + 1) * L)
        tok = lambda ref: [ref[rows, sl] for sl in lanes]
        r, v, k_raw = tok(r_ref), tok(v_ref), tok(k_ref)
        lw = each(lambda w0, wl: -math.exp(-0.5) * jax.nn.sigmoid(w0 + wl), par(w0_ref), tok(wl_ref))
        a = each(lambda a0, al: jax.nn.sigmoid(a0 + al), par(a0_ref), tok(al_ref))
        kk = each(lambda kx, kkp: kx * kkp, k_raw, par(kk_ref))
        k = each(lambda kx, ax, ka: kx * (1.0 + (ax - 1.0) * ka), k_raw, a, par(ka_ref))
        kk = each(lambda x, sq: x / jnp.maximum(jnp.sqrt(sq), 1e-12), kk, head_sums([x * x for x in kk]))
        cum = each(cumsum, lw)
        cum_last = [cx[L - 1:L] for cx in cum]
        p_inv = each(lambda cx: jnp.exp(-cx), cum)
        decay = [jnp.exp(cl) for cl in cum_last]
        p_rest = each(lambda pi, dx: pi * dx, p_inv, decay)
        b = each(lambda x, ax: x * ax, kk, a)
        lhs_ar = each(lambda x, cx, lx, rx: stack(-x * jnp.exp(cx - lx), rx * jnp.exp(cx)), kk, cum, lw, r)
        rkr = each(lambda rx, kx, rk: rx * kx * rk, r, k, par(rk_ref))
        bonus = each(lambda sx, vx: sx * vx, head_sums(rkr), v)
        kb_rest =each(lambda bx, kx, pr: stack(bx * pr, kx * pr), b, k, p_rest)
        b_w = each(lambda bx, pi: block_diag(bx * pi), b, p_inv)
        k_w = each(lambda kx, pi: block_diag(kx * pi), k, p_inv)
        ab = each(_dot_nt, lhs_ar, b_w)
        a_ab = [jnp.where(strict, x[:L], 0.0) for x in ab]
        a_rb = [jnp.where(incl, x[L:], 0.0) for x in ab]
        inv = [eye + x for x in a_ab]
        npow = [x.astype(BF16) for x in a_ab]
        npow = each(lambda n: _dot(n, block_diag(n)).astype(BF16), npow)
        for _ in range(4):
            res = each(lambda n, t: _dot(jnp.concatenate([n, t.astype(BF16)], axis=0), block_diag(n)), npow, inv)
            npow = [x[:L].astype(BF16) for x in res]
            inv = [t + x[L:] for t, x in zip(inv, res)]
        inv = each(lambda t, n: t + _dot(t.astype(BF16), block_diag(n)), inv, npow)
        ak = each(_dot_nt, lhs_ar, k_w)
        a_ak = [jnp.where(strict, x[:L], 0.0) for x in ak]
        a_rk = [jnp.where(incl, x[L:], 0.0) for x in ak]
        a_v = each(lambda x, y, vx: _dot(stack(x, y), block_diag(vx)), a_ak, a_rk, v)
        return dict(rows=rows, v=v, lhs_ar=lhs_ar, inv=inv, a_v=a_v, a_rb=a_rb, bonus=bonus,
                    decay=decay, kb_rest=kb_rest, gate=tok(g_ref))

    prepared = [prepare(c) for c in range(chunks)]
    gn_w, gn_b = par(gnw_ref), par(gnb_ref)

    state = [s_ref[gi] for gi in range(groups)]
    for q in prepared:
        a_s = each(lambda lhs, s: _dot_nt(lhs, s.astype(BF16)), q["lhs_ar"], state)
        z = each(lambda t, av, as_: _dot(t.astype(BF16), block_diag(av[:L] + as_[:L])), q["inv"], q["a_v"], a_s)
        y = each(lambda as_, av, arb, zx: as_[L:] + av[L:] + _dot(arb.astype(BF16), block_diag(zx)),
                 a_s, q["a_v"], q["a_rb"], z)
        upd = each(lambda zx, vx, kb: _dot_tn(stack(zx, vx), kb), z, q["v"], q["kb_rest"])
        state = each(lambda s, dx, ux: jnp.where(bd, s * dx + ux, 0.0), state, q["decay"], upd)

        mean = [m * (1.0 / N) for m in head_sums(y)]
        yc = each(lambda yx, mx: yx - mx, y, mean)
        var = [m * (1.0 / N) for m in head_sums([x * x for x in yc])]
        for gi, sl in enumerate(lanes):
            yn = yc[gi] * lax.rsqrt(var[gi] + GN_EPS) * gn_w[gi] + gn_b[gi]
            o_ref[q["rows"], sl] = ((yn + q["bonus"][gi]) * q["gate"][gi]).astype(o_ref.dtype)
    for gi in range(groups):
        s_ref[gi] = state[gi]


def _wkv(r, k, v, wl, al, g, w0, a0, k_k, k_a, r_k, gn_w, gn_b, batch, seq, groups=8, chunks=2):
    m, d = r.shape
    rows = chunks * WKV_CHUNK
    gw = groups * MXU_DIM
    ng = d // gw
    nc = seq // rows
    tok = pl.BlockSpec((rows, gw), lambda b, gidx, c: (b * nc + c, gidx))
    par = pl.BlockSpec((1, gw), lambda b, gidx, c: (0, gidx))
    row = lambda t: t.reshape(1, d)
    return pl.pallas_call(
        functools.partial(_wkv_kernel, groups=groups, chunks=chunks),
        grid=(batch, ng, nc),
        in_specs=[tok] * 6 + [par] * 7,
        out_specs=tok,
        out_shape=jax.ShapeDtypeStruct((m, d), BF16),
        scratch_shapes=[pltpu.VMEM((groups, MXU_DIM, MXU_DIM), F32)],
        compiler_params=_params("parallel", "parallel", "arbitrary"),
        name="wkv7",
    )(r, k, v, wl, al, g, row(w0), row(a0), row(k_k), row(k_a), row(r_k), row(gn_w), row(gn_b))


def _pad_to(x, axis, size):
    pad = [(0, 0)] * x.ndim
    pad[axis] = (0, size - x.shape[axis])
    return jnp.pad(x, pad)


def _round_up(n, mult):
    return -(-n // mult) * mult


def kernel(x, c, mod_w, mod_b, norm_g, fox_w_in, fox_b_f, fox_w_out, rwkv_mu, rwkv_w0, rwkv_w1, rwkv_w2, rwkv_a0, rwkv_a1, rwkv_a2, rwkv_g1, rwkv_g2, rwkv_k_k, rwkv_k_a, rwkv_r_k, rwkv_w_r, rwkv_w_k, rwkv_w_v, rwkv_w_o, rwkv_gn_w, rwkv_gn_b, ffn_w_in, ffn_conv_w, ffn_conv_b, ffn_w_out):
    batch, seq, d = x.shape
    depth = mod_w.shape[0]
    heads = fox_b_f.shape[1]
    bf = lambda t: t.astype(BF16)

    mod = _modulation(c, mod_w, mod_b)
    x2 = x.reshape(batch * seq, d)

    h = None
    for i in range(depth):
        sh_m, sc_m, g_m, sh_f, sc_f, g_f = [mod[i, :, n] for n in range(6)]
        j = i // 2
        ffn_pre = (norm_g[i, 2], sc_f, sh_f)
        if i % 2 == 0:
            if h is None:
                h = _prenorm(x2, norm_g[i, 0], sc_m, sh_m, seq)
            qkv = _matmul(h, fox_w_in, j, 3 * d, BF16, name="fox_qkv")
            wf = _pad_to(bf(fox_w_in[j, :, 3 * d:]), 1, LANES)
            bfp = _pad_to(fox_b_f[j].reshape(1, heads), 1, LANES)
            cum = _fox_gate(h, wf, bfp, batch, seq)
            o = _fox_attention(qkv, cum, batch, seq, heads)
            x2, h = _matmul_postnorm(o, fox_w_out, x2, norm_g[i, 1], g_m, seq, nxt=ffn_pre, layer=j)
        else:
            xr, xw, xk, xv, xa, xg = _rwkv_pre(x2, norm_g[i, 0], sc_m, sh_m, rwkv_mu[j], seq)
            r = _matmul(xr, rwkv_w_r, j, d, F32, name="rwkv_r")
            k = _matmul(xk, rwkv_w_k, j, d, F32, name="rwkv_k")
            v = _matmul(xv, rwkv_w_v, j, d, F32, name="rwkv_v")
            lr = _round_up(rwkv_w1.shape[2], LANES)
            wl = _lora(xw, _pad_to(bf(rwkv_w1[j]), 1, lr), _pad_to(bf(rwkv_w2[j]), 0, lr), "tanh")
            al = _lora(xa, _pad_to(bf(rwkv_a1[j]), 1, lr), _pad_to(bf(rwkv_a2[j]), 0, lr), "none")
            gr = _round_up(rwkv_g1.shape[2], LANES)
            g = _lora(xg, _pad_to(bf(rwkv_g1[j]), 1, gr), _pad_to(bf(rwkv_g2[j]), 0, gr), "sigmoid")
            o = _wkv(r, k, v, wl, al, g, rwkv_w0[j], rwkv_a0[j], rwkv_k_k[j], rwkv_k_a[j],
                     rwkv_r_k[j], rwkv_gn_w[j], rwkv_gn_b[j], batch, seq)
            x2, h = _matmul_postnorm(o, rwkv_w_o, x2, norm_g[i, 1], g_m, seq, nxt=ffn_pre, layer=j)
        act = _ffn_in(h, ffn_w_in, ffn_conv_w, ffn_conv_b.reshape(depth, 1, -1), i, seq)
        f = act.shape[1]
        nxt = None
        if i + 1 < depth and (i + 1) % 2 == 0:
            nxt = (norm_g[i + 1, 0], mod[i + 1, :, 1], mod[i + 1, :, 0])
        x2, h = _matmul_postnorm(act, bf(ffn_w_out[i]), x2, norm_g[i, 3], g_f, seq, nxt=nxt, tk=f // 2)
    return x2.reshape(batch, seq, d)
```

```python
import functools
import math

import jax
import jax.numpy as jnp
from jax import lax
from jax.experimental import pallas as pl
from jax.experimental.pallas import tpu as pltpu

F32 = jnp.float32
BF16 = jnp.bfloat16

LOG2E = math.log2(math.e)
NORM_EPS = 1e-6
GN_EPS = 64e-5
FOX_HEADS = 16
RWKV_HEAD_DIM = 64
CONV_WIDTH = 3

LANES = 128
SUBLANES = 8
MXU_DIM = 256
WKV_CHUNK = 64
WKV_HEADS_PER_GROUP = MXU_DIM // RWKV_HEAD_DIM
VMEM_LIMIT = 56 << 20


def _params(*sem):
    return pltpu.CompilerParams(dimension_semantics=sem, vmem_limit_bytes=VMEM_LIMIT)


def _inv_rms(x):
    return lax.rsqrt(jnp.mean(x * x, axis=-1, keepdims=True) + NORM_EPS)


def _rms(x, g):
    return x * _inv_rms(x) * g


def _dot(a, b):
    return jnp.dot(a, b, preferred_element_type=F32)


def _dot_nt(a, b):
    return lax.dot_general(a, b, (((1,), (1,)), ((), ())), preferred_element_type=F32)


def _dot_tn(a, b):
    return lax.dot_general(a, b, (((0,), (0,)), ((), ())), preferred_element_type=F32)


def _split2(x):
    hi = x.astype(BF16)
    lo = (x - hi.astype(F32)).astype(BF16)
    return hi, lo


def _split3(x):
    hi = x.astype(BF16)
    r1 = x - hi.astype(F32)
    mid = r1.astype(BF16)
    lo = (r1 - mid.astype(F32)).astype(BF16)
    return hi, mid, lo


def _softplus(x):
    return jnp.maximum(x, 0.0) + jnp.log1p(jnp.exp(-jnp.abs(x)))


def _mod_kernel(c_ref, w_ref, b_ref, o_ref):
    c = c_ref[...]
    s = c * jax.nn.sigmoid(c)
    o_ref[0] = _dot(s.astype(BF16), w_ref[0].astype(BF16)) + b_ref[0]


def _modulation(c, mod_w, mod_b):
    depth, d, n = mod_w.shape
    b = c.shape[0]
    rows = -(-b // SUBLANES) * SUBLANES
    c_pad = jnp.pad(c, ((0, rows - b), (0, 0)))
    tn = 1024
    out = pl.pallas_call(
        _mod_kernel,
        grid=(depth, n // tn),
        in_specs=[pl.BlockSpec((rows, d), lambda l, j: (0, 0)),
                  pl.BlockSpec((1, d, tn), lambda l, j: (l, 0, j)),
                  pl.BlockSpec((1, 1, tn), lambda l, j: (l, 0, j))],
        out_specs=pl.BlockSpec((1, rows, tn), lambda l, j: (l, 0, j)),
        out_shape=jax.ShapeDtypeStruct((depth, rows, n), F32),
        compiler_params=_params("parallel", "parallel"),
        name="adaln_mod",
    )(c_pad, mod_w, mod_b.reshape(depth, 1, n))
    return out[:, :b].reshape(depth, b, 6, 1, d)


def _prenorm_kernel(x_ref, g_ref, sc_ref, sh_ref, o_ref):
    inv = _inv_rms(x_ref[...])
    for c in range(x_ref.shape[1] // MXU_DIM):
        cols = slice(c * MXU_DIM, (c + 1) * MXU_DIM)
        h = x_ref[:, cols] * inv * g_ref[:, cols] * (1.0 + sc_ref[0, :, cols]) + sh_ref[0, :, cols]
        o_ref[:, cols] = h.astype(o_ref.dtype)


def _prenorm(x2, g, sc, sh, seq, tm=512):
    m, d = x2.shape
    tpb = seq // tm
    return pl.pallas_call(
        _prenorm_kernel,
        grid=(m // tm,),
        in_specs=[pl.BlockSpec((tm, d), lambda i: (i, 0)),
                  pl.BlockSpec((1, d), lambda i: (0, 0)),
                  pl.BlockSpec((1, 1, d), lambda i: (i // tpb, 0, 0)),
                  pl.BlockSpec((1, 1, d), lambda i: (i // tpb, 0, 0))],
        out_specs=pl.BlockSpec((tm, d), lambda i: (i, 0)),
        out_shape=jax.ShapeDtypeStruct((m, d), BF16),
        compiler_params=_params("parallel"),
        name="prenorm",
    )(x2, g.reshape(1, d), sc, sh)


def _mm_kernel(a_ref, w_ref, o_ref, wb_ref):
    @pl.when(pl.program_id(1) == 0)
    def _():
        wb_ref[...] = w_ref[...].astype(BF16)

    o_ref[...] = _dot(a_ref[...], wb_ref[...]).astype(o_ref.dtype)


def _matmul(a, w, layer, n, out_dtype, tm=2048, tn=512, name="matmul"):
    m, k = a.shape
    return pl.pallas_call(
        _mm_kernel,
        grid=(n // tn, m // tm),
        in_specs=[pl.BlockSpec((tm, k), lambda j, i: (i, 0)),
                  pl.BlockSpec((None, k, tn), lambda j, i: (layer, 0, j))],
        out_specs=pl.BlockSpec((tm, tn), lambda j, i: (i, j)),
        out_shape=jax.ShapeDtypeStruct((m, n), out_dtype),
        scratch_shapes=[pltpu.VMEM((k, tn), BF16)],
        compiler_params=_params("parallel", "arbitrary"),
        name=name,
    )(a, w)


def _out_kernel(*refs, nk, with_h, cast_w, sub):
    a_ref, w_ref, x_ref, g1_ref, gate_ref = refs[:5]
    if with_h:
        g2_ref, sc_ref, sh_ref, xo_ref, ho_ref = refs[5:10]
    else:
        xo_ref = refs[5]
    acc_ref = xo_ref
    if cast_w:
        wb_ref = refs[-1]

        @pl.when(pl.program_id(0) == 0)
        def _():
            wb_ref[...] = w_ref[...].astype(BF16)

        w_ref = wb_ref

    def epilogue(rows, y):
        xn = x_ref[rows, :] + gate_ref[0] * _rms(y, g1_ref[...])
        xo_ref[rows, :] = xn
        if with_h:
            h = _rms(xn, g2_ref[...]) * (1.0 + sc_ref[0]) + sh_ref[0]
            ho_ref[rows, :] = h.astype(ho_ref.dtype)

    def finish():
        tm = a_ref.shape[0]
        tiles = [slice(r, r + sub) for r in range(0, tm, sub)]

        def total(rows):
            part = _dot(a_ref[rows, :], w_ref[...])
            return part if nk == 1 else acc_ref[rows, :] + part

        y = total(tiles[0])
        for n, rows in enumerate(tiles):
            y_next = total(tiles[n + 1]) if n + 1 < len(tiles) else None
            epilogue(rows, y)
            y = y_next

    if nk == 1:
        finish()
        return
    k = pl.program_id(1)

    def accumulate(first):
        for r in range(0, a_ref.shape[0], sub):
            part = _dot(a_ref[r:r + sub, :], w_ref[...])
            acc_ref[r:r + sub, :] = part if first else acc_ref[r:r + sub, :] + part

    pl.when(k == 0)(functools.partial(accumulate, True))
    pl.when(jnp.logical_and(k > 0, k < nk - 1))(functools.partial(accumulate, False))
    pl.when(k == nk - 1)(finish)


def _matmul_postnorm(a, w, layer, x2, g1, gate, seq, nxt=None, tm=512, tk=None, sub=128):
    m, kdim = a.shape
    d = w.shape[-1]
    cast_w = w.dtype != BF16
    tk = kdim if tk is None else tk
    nk = kdim // tk
    assert not (cast_w and nk > 1)
    tpb = seq // tm
    with_h = nxt is not None
    row = lambda i, k: (i, 0)
    per_batch = lambda i, k: (i // tpb, 0, 0)
    const = lambda i, k: (0, 0)
    if cast_w:
        w_spec = pl.BlockSpec((None, kdim, d), lambda i, k: (layer, 0, 0), pipeline_mode=pl.Buffered(1))
    else:
        w_spec = pl.BlockSpec((None, tk, d), lambda i, k: (layer, k, 0))
    in_specs = [pl.BlockSpec((tm, tk), lambda i, k: (i, k)),
                w_spec,
                pl.BlockSpec((tm, d), row),
                pl.BlockSpec((1, d), const),
                pl.BlockSpec((1, 1, d), per_batch)]
    args = [a, w, x2, g1.reshape(1, d), gate]
    out_specs = [pl.BlockSpec((tm, d), row)]
    out_shape = [jax.ShapeDtypeStruct((m, d), F32)]
    if with_h:
        g2, sc, sh = nxt
        in_specs += [pl.BlockSpec((1, d), const), pl.BlockSpec((1, 1, d), per_batch),
                     pl.BlockSpec((1, 1, d), per_batch)]
        args += [g2.reshape(1, d), sc, sh]
        out_specs.append(pl.BlockSpec((tm, d), row))
        out_shape.append(jax.ShapeDtypeStruct((m, d), BF16))
    res = pl.pallas_call(
        functools.partial(_out_kernel, nk=nk, with_h=with_h, cast_w=cast_w, sub=sub),
        grid=(m // tm, nk),
        in_specs=in_specs,
        out_specs=out_specs,
        out_shape=out_shape,
        scratch_shapes=[pltpu.VMEM((kdim, d), BF16)] if cast_w else [],
        compiler_params=_params("arbitrary", "arbitrary"),
        name="matmul_postnorm",
    )(*args)
    return (res[0], res[1]) if with_h else (res[0], None)


def _fox_gate_kernel(h_ref, wf_ref, bf_ref, o_ref, carry_ref, *, ts):
    @pl.when(pl.program_id(1) == 0)
    def _():
        carry_ref[...] = jnp.zeros_like(carry_ref)

    logit = _dot(h_ref[...], wf_ref[...]) + bf_ref[...]
    log_f = -_softplus(-logit)
    row = lax.broadcasted_iota(jnp.int32, (ts, ts), 0)
    col = lax.broadcasted_iota(jnp.int32, (ts, ts), 1)
    tri = jnp.where(row >= col, 1.0, 0.0).astype(BF16)
    hi, mid, lo = _split3(log_f)
    cs = _dot(tri, hi) + _dot(tri, mid) + _dot(tri, lo) + carry_ref[...]
    o_ref[...] = cs
    carry_ref[...] = cs[ts - 1:ts, :]


def _fox_gate(h, wf_pad, bf_pad, batch, seq, ts=256):
    m, d = h.shape
    spb = seq // ts
    return pl.pallas_call(
        functools.partial(_fox_gate_kernel, ts=ts),
        grid=(batch, spb),
        in_specs=[pl.BlockSpec((ts, d), lambda b, s: (b * spb + s, 0)),
                  pl.BlockSpec((d, LANES), lambda b, s: (0, 0)),
                  pl.BlockSpec((1, LANES), lambda b, s: (0, 0))],
        out_specs=pl.BlockSpec((ts, LANES), lambda b, s: (b * spb + s, 0)),
        out_shape=jax.ShapeDtypeStruct((m, LANES), F32),
        scratch_shapes=[pltpu.VMEM((1, LANES), F32)],
        compiler_params=_params("parallel", "arbitrary"),
        name="fox_gate",
    )(h, wf_pad, bf_pad)


def _fox_attn_kernel(q_ref, k_ref, v_ref, cq_ref, ck_ref, o_ref, *, tq, tk, scale):
    head = pl.program_id(1)
    i = pl.program_id(2)
    ratio = tq // tk
    hd = q_ref.shape[1]
    lane = lax.broadcasted_iota(jnp.int32, (tq, LANES), 1)
    cq = jnp.sum(jnp.where(lane == head, cq_ref[...], 0.0), axis=1, keepdims=True) * LOG2E
    row_id = lax.broadcasted_iota(jnp.int32, (tk, tk), 0)
    col_id = lax.broadcasted_iota(jnp.int32, (tk, tk), 1)

    def update(j, row0, masked, carry):
        start = pl.multiple_of(j * tk, tk)
        kj = k_ref[pl.ds(start, tk), :]
        vj = v_ref[pl.ds(start, tk), :]
        ckj = ck_ref[0, 0, j] * LOG2E
        m_prev, l_prev, acc = carry
        s = _dot_nt(q_ref[row0:, :], kj) * (scale * LOG2E) + cq[row0:] - ckj
        if masked:
            top = jnp.where(row_id >= col_id, s[:tk], -jnp.inf)
            s = top if tq - row0 == tk else jnp.concatenate([top, s[tk:]], axis=0)
        m_new = jnp.maximum(m_prev, jnp.max(s, axis=1, keepdims=True))
        alpha = jnp.exp2(m_prev - m_new)
        p = jnp.exp2(s - m_new)
        l_new = alpha * l_prev + jnp.sum(p, axis=1, keepdims=True)
        return m_new, l_new, alpha * acc + _dot(p.astype(BF16), vj)

    init = (jnp.full((tq, 1), -jnp.inf, F32), jnp.zeros((tq, 1), F32), jnp.zeros((tq, hd), F32))
    carry = lax.fori_loop(0, ratio * i, lambda j, c: update(j, 0, False, c), init)
    done = []
    for t in range(ratio):
        carry = update(ratio * i + t, t * tk, True, carry)
        done.append(tuple(x[:tk] for x in carry))
        if t + 1 < ratio:
            carry = tuple(x[tk:] for x in carry)
    l_fin = jnp.concatenate([d[1] for d in done], axis=0)
    acc = jnp.concatenate([d[2] for d in done], axis=0)
    o_ref[...] = (acc / l_fin).astype(o_ref.dtype)


def _fox_attention(qkv, cum, batch, seq, heads, tq=2048, tk=256):
    m = qkv.shape[0]
    hd = qkv.shape[1] // (3 * heads)
    nq = seq // tq
    nk = seq // tk
    ck = cum[:, :heads].reshape(batch, seq, heads).transpose(0, 2, 1).reshape(batch, heads, nk, 1, tk)
    return pl.pallas_call(
        functools.partial(_fox_attn_kernel, tq=tq, tk=tk, scale=hd ** -0.5),
        grid=(batch, heads, nq),
        in_specs=[pl.BlockSpec((tq, hd), lambda b, h, i: (b * nq + i, h)),
                  pl.BlockSpec((seq, hd), lambda b, h, i: (b, heads + h)),
                  pl.BlockSpec((seq, hd), lambda b, h, i: (b, 2 * heads + h)),
                  pl.BlockSpec((tq, LANES), lambda b, h, i: (b * nq + i, 0)),
                  pl.BlockSpec((1, 1, nk, 1, tk), lambda b, h, i: (b, h, 0, 0, 0))],
        out_specs=pl.BlockSpec((tq, hd), lambda b, h, i: (b * nq + i, h)),
        out_shape=jax.ShapeDtypeStruct((m, heads * hd), BF16),
        compiler_params=_params("parallel", "parallel", "arbitrary"),
        name="fox_attention",
    )(qkv, qkv, qkv, cum, ck)


def _ffn_in_kernel(h_ref, w1_ref, w2_ref, cw1_ref, cw2_ref, cb1_ref, cb2_ref, o_ref, tail_ref, wb_ref,
                   *, tm, sub, tpb):
    @pl.when(pl.program_id(1) % tpb == 0)
    def _():
        tail_ref[...] = jnp.zeros_like(tail_ref)

    @pl.when(pl.program_id(1) == 0)
    def _():
        wb_ref[0] = w1_ref[...].astype(BF16)
        wb_ref[1] = w2_ref[...].astype(BF16)

    row = lax.broadcasted_iota(jnp.int32, (SUBLANES, 1), 0)
    w_refs = (wb_ref.at[0], wb_ref.at[1])
    cw = (cw1_ref[...], cw2_ref[...])
    cb = (cb1_ref[...], cb2_ref[...])

    def up(r):
        rows = h_ref[r * sub:(r + 1) * sub, :]
        return [_dot(rows, w_ref[...]) for w_ref in w_refs]

    def shifted(u, tail, shift):
        rolled = pltpu.roll(u, shift, 0)
        head = rolled[:SUBLANES]
        for s in range(shift):
            src = SUBLANES - shift + s
            head = jnp.where(row == s, tail[src:src + 1], head)
        return jnp.concatenate([head, rolled[SUBLANES:]], axis=0)

    def conv(n, u, tail):
        return cw[n][0:1] * shifted(u, tail, 2) + cw[n][1:2] * shifted(u, tail, 1) + cw[n][2:3] * u + cb[n]

    nsub = tm // sub
    tails = [tail_ref[0], tail_ref[1]]
    u = up(0)
    for r in range(nsub):
        u_next = up(r + 1) if r + 1 < nsub else None
        x1 = conv(0, u[0], tails[0])
        x2 = conv(1, u[1], tails[1])
        gelu = 0.5 * x1 * (1.0 + lax.erf(x1 * (2.0 ** -0.5)))
        o_ref[r * sub:(r + 1) * sub, :] = (gelu * x2).astype(o_ref.dtype)
        tails = [x[sub - SUBLANES:] for x in u]
        u = u_next
    tail_ref[0] = tails[0]
    tail_ref[1] = tails[1]


def _ffn_in(h, w_in, conv_w, conv_b, layer, seq, tm=2048, tn=512, sub=128):
    m, d = h.shape
    f = w_in.shape[2] // 2
    nf = f // tn
    tpb = seq // tm
    return pl.pallas_call(
        functools.partial(_ffn_in_kernel, tm=tm, sub=sub, tpb=tpb),
        grid=(nf, m // tm),
        in_specs=[pl.BlockSpec((tm, d), lambda j, i: (i, 0)),
                  pl.BlockSpec((None, d, tn), lambda j, i: (layer, 0, j)),
                  pl.BlockSpec((None, d, tn), lambda j, i: (layer, 0, nf + j)),
                  pl.BlockSpec((None, CONV_WIDTH, tn), lambda j, i: (layer, 0, j)),
                  pl.BlockSpec((None, CONV_WIDTH, tn), lambda j, i: (layer, 0, nf + j)),
                  pl.BlockSpec((None, 1, tn), lambda j, i: (layer, 0, j)),
                  pl.BlockSpec((None, 1, tn), lambda j, i: (layer, 0, nf + j))],
        out_specs=pl.BlockSpec((tm, tn), lambda j, i: (i, j)),
        out_shape=jax.ShapeDtypeStruct((m, f), BF16),
        scratch_shapes=[pltpu.VMEM((2, SUBLANES, tn), F32), pltpu.VMEM((2, d, tn), BF16)],
        compiler_params=_params("parallel", "arbitrary"),
        name="ffn_in",
    )(h, w_in, w_in, conv_w, conv_w, conv_b, conv_b)


def _rwkv_pre_kernel(x_ref, halo_ref, g_ref, sc_ref, sh_ref, mu_ref, *o_refs, tm, tpb):
    first = pl.program_id(0) % tpb == 0
    inv = _inv_rms(x_ref[...])
    inv_halo = _inv_rms(halo_ref[...])
    row = lax.broadcasted_iota(jnp.int32, (SUBLANES, 1), 0)
    for c in range(x_ref.shape[1] // MXU_DIM):
        cols = slice(c * MXU_DIM, (c + 1) * MXU_DIM)
        mod = lambda t, s: t * s * g_ref[:, cols] * (1.0 + sc_ref[0, :, cols]) + sh_ref[0, :, cols]
        h = mod(x_ref[:, cols], inv)
        h_halo = jnp.where(first, 0.0, mod(halo_ref[:, cols], inv_halo))
        rolled = pltpu.roll(h, 1, 0)
        head = jnp.where(row == 0, h_halo[SUBLANES - 1:SUBLANES], rolled[:SUBLANES])
        xx = jnp.concatenate([head, rolled[SUBLANES:]], axis=0) - h
        for n, o_ref in enumerate(o_refs):
            o_ref[:, cols] = (h + xx * mu_ref[n:n + 1, cols]).astype(o_ref.dtype)


def _rwkv_pre(x2, g, sc, sh, mu, seq, tm=256):
    m, d = x2.shape
    tpb = seq // tm
    hb = tm // SUBLANES
    nmix = mu.shape[0]
    return pl.pallas_call(
        functools.partial(_rwkv_pre_kernel, tm=tm, tpb=tpb),
        grid=(m // tm,),
        in_specs=[pl.BlockSpec((tm, d), lambda i: (i, 0)),
                  pl.BlockSpec((SUBLANES, d), lambda i: (jnp.maximum(i * hb - 1, 0), 0)),
                  pl.BlockSpec((1, d), lambda i: (0, 0)),
                  pl.BlockSpec((1, 1, d), lambda i: (i // tpb, 0, 0)),
                  pl.BlockSpec((1, 1, d), lambda i: (i // tpb, 0, 0)),
                  pl.BlockSpec((nmix, d), lambda i: (0, 0))],
        out_specs=[pl.BlockSpec((tm, d), lambda i: (i, 0))] * nmix,
        out_shape=[jax.ShapeDtypeStruct((m, d), BF16)] * nmix,
        compiler_params=_params("parallel"),
        name="rwkv_pre",
    )(x2, x2, g.reshape(1, d), sc, sh, mu)


def _lora_kernel(x_ref, w1_ref, w2_ref, o_ref, *, act):
    t = _dot(x_ref[...], w1_ref[...])
    if act == "tanh":
        t = jnp.tanh(t)
    elif act == "sigmoid":
        t = jax.nn.sigmoid(t)
    o_ref[...] = _dot(t.astype(BF16), w2_ref[...])


def _lora(x, w1, w2, act, tm=512):
    m, d = x.shape
    r = w1.shape[1]
    n = w2.shape[1]
    return pl.pallas_call(
        functools.partial(_lora_kernel, act=act),
        grid=(m // tm,),
        in_specs=[pl.BlockSpec((tm, d), lambda i: (i, 0)),
                  pl.BlockSpec((d, r), lambda i: (0, 0)),
                  pl.BlockSpec((r, n), lambda i: (0, 0))],
        out_specs=pl.BlockSpec((tm, n), lambda i: (i, 0)),
        out_shape=jax.ShapeDtypeStruct((m, n), F32),
        compiler_params=_params("parallel"),
        name="rwkv_lora_" + act,
    )(x, w1, w2)


def _wkv_kernel(r_ref, k_ref, v_ref, wl_ref, al_ref, g_ref,
                w0_ref, a0_ref, kk_ref, ka_ref, rk_ref, gnw_ref, gnb_ref,
                o_ref, s_ref, *, groups, chunks):
    L = WKV_CHUNK
    N = RWKV_HEAD_DIM
    W = MXU_DIM

    @pl.when(pl.program_id(2) == 0)
    def _():
        s_ref[...] = jnp.zeros_like(s_ref)

    ri = lax.broadcasted_iota(jnp.int32, (W, W), 0)
    ci = lax.broadcasted_iota(jnp.int32, (W, W), 1)
    bd = (ri // N) == (ci // N)
    bd_bf = jnp.where(bd, 1.0, 0.0).astype(BF16)
    t_i = lax.broadcasted_iota(jnp.int32, (L, W), 0)
    s_i = lax.broadcasted_iota(jnp.int32, (L, W), 1) % N
    strict = t_i > s_i
    incl = t_i >= s_i
    eye = jnp.where(t_i == s_i, 1.0, 0.0)
    tri = jnp.where(lax.broadcasted_iota(jnp.int32, (L, L), 0) >= lax.broadcasted_iota(jnp.int32, (L, L), 1),
                    1.0, 0.0).astype(BF16)

    def block_diag(x):
        xb = x.astype(BF16)
        return jnp.concatenate([xb] * (W // L), axis=0) * bd_bf

    def stack(a, b):
        return jnp.concatenate([a, b], axis=0).astype(BF16)

    def head_sums(xs):
        res = _dot(jnp.concatenate(xs, axis=0).astype(BF16), bd_bf)
        return [res[n * L:(n + 1) * L] for n in range(len(xs))]

    def each(fn, *cols):
        return [fn(*args) for args in zip(*cols)]

    def cumsum(x):
        hi, mid, lo = _split3(x)
        return _dot(tri, hi) + _dot(tri, mid) + _dot(tri, lo)

    lanes = [slice(gi * W, (gi + 1) * W) for gi in range(groups)]

    def par(ref):
        return [ref[:, sl] for sl in lanes]

    def prepare(c):
        rows = slice(c * L, (c + 1) * L)
        tok = lambda ref: [ref[rows, sl] for sl in lanes]
        r, v, k_raw = tok(r_ref), tok(v_ref), tok(k_ref)
        lw = each(lambda w0, wl: -math.exp(-0.5) * jax.nn.sigmoid(w0 + wl), par(w0_ref), tok(wl_ref))
        a = each(lambda a0, al: jax.nn.sigmoid(a0 + al), par(a0_ref), tok(al_ref))
        kk = each(lambda kx, kkp: kx * kkp, k_raw, par(kk_ref))
        k = each(lambda kx, ax, ka: kx * (1.0 + (ax - 1.0) * ka), k_raw, a, par(ka_ref))
        kk = each(lambda x, sq: x / jnp.maximum(jnp.sqrt(sq), 1e-12), kk, head_sums([x * x for x in kk]))
        cum = each(cumsum, lw)
        cum_last = [cx[L - 1:L] for cx in cum]
        p_inv = each(lambda cx: jnp.exp(-cx), cum)
        decay = [jnp.exp(cl) for cl in cum_last]
        p_rest = each(lambda pi, dx: pi * dx, p_inv, decay)
        b = each(lambda x, ax: x * ax, kk, a)
        lhs_ar = each(lambda x, cx, lx, rx: stack(-x * jnp.exp(cx - lx), rx * jnp.exp(cx)), kk, cum, lw, r)
        rkr = each(lambda rx, kx, rk: rx * kx * rk, r, k, par(rk_ref))
        bonus = each(lambda sx, vx: sx * vx, head_sums(rkr), v)
        kb_rest =each(lambda bx, kx, pr: stack(bx * pr, kx * pr), b, k, p_rest)
        b_w = each(lambda bx, pi: block_diag(bx * pi), b, p_inv)
        k_w = each(lambda kx, pi: block_diag(kx * pi), k, p_inv)
        ab = each(_dot_nt, lhs_ar, b_w)
        a_ab = [jnp.where(strict, x[:L], 0.0) for x in ab]
        a_rb = [jnp.where(incl, x[L:], 0.0) for x in ab]
        inv = [eye + x for x in a_ab]
        npow = [x.astype(BF16) for x in a_ab]
        npow = each(lambda n: _dot(n, block_diag(n)).astype(BF16), npow)
        for _ in range(4):
            res = each(lambda n, t: _dot(jnp.concatenate([n, t.astype(BF16)], axis=0), block_diag(n)), npow, inv)
            npow = [x[:L].astype(BF16) for x in res]
            inv = [t + x[L:] for t, x in zip(inv, res)]
        inv = each(lambda t, n: t + _dot(t.astype(BF16), block_diag(n)), inv, npow)
        ak = each(_dot_nt, lhs_ar, k_w)
        a_ak = [jnp.where(strict, x[:L], 0.0) for x in ak]
        a_rk = [jnp.where(incl, x[L:], 0.0) for x in ak]
        a_v = each(lambda x, y, vx: _dot(stack(x, y), block_diag(vx)), a_ak, a_rk, v)
        return dict(rows=rows, v=v, lhs_ar=lhs_ar, inv=inv, a_v=a_v, a_rb=a_rb, bonus=bonus,
                    decay=decay, kb_rest=kb_rest, gate=tok(g_ref))

    prepared = [prepare(c) for c in range(chunks)]
    gn_w, gn_b = par(gnw_ref), par(gnb_ref)

    state = [s_ref[gi] for gi in range(groups)]
    for q in prepared:
        a_s = each(lambda lhs, s: _dot_nt(lhs, s.astype(BF16)), q["lhs_ar"], state)
        z = each(lambda t, av, as_: _dot(t.astype(BF16), block_diag(av[:L] + as_[:L])), q["inv"], q["a_v"], a_s)
        y = each(lambda as_, av, arb, zx: as_[L:] + av[L:] + _dot(arb.astype(BF16), block_diag(zx)),
                 a_s, q["a_v"], q["a_rb"], z)
        upd = each(lambda zx, vx, kb: _dot_tn(stack(zx, vx), kb), z, q["v"], q["kb_rest"])
        state = each(lambda s, dx, ux: jnp.where(bd, s * dx + ux, 0.0), state, q["decay"], upd)

        mean = [m * (1.0 / N) for m in head_sums(y)]
        yc = each(lambda yx, mx: yx - mx, y, mean)
        var = [m * (1.0 / N) for m in head_sums([x * x for x in yc])]
        for gi, sl in enumerate(lanes):
            yn = yc[gi] * lax.rsqrt(var[gi] + GN_EPS) * gn_w[gi] + gn_b[gi]
            o_ref[q["rows"], sl] = ((yn + q["bonus"][gi]) * q["gate"][gi]).astype(o_ref.dtype)
    for gi in range(groups):
        s_ref[gi] = state[gi]


def _wkv(r, k, v, wl, al, g, w0, a0, k_k, k_a, r_k, gn_w, gn_b, batch, seq, groups=8, chunks=2):
    m, d = r.shape
    rows = chunks * WKV_CHUNK
    gw = groups * MXU_DIM
    ng = d // gw
    nc = seq // rows
    tok = pl.BlockSpec((rows, gw), lambda b, gidx, c: (b * nc + c, gidx))
    par = pl.BlockSpec((1, gw), lambda b, gidx, c: (0, gidx))
    row = lambda t: t.reshape(1, d)
    return pl.pallas_call(
        functools.partial(_wkv_kernel, groups=groups, chunks=chunks),
        grid=(batch, ng, nc),
        in_specs=[tok] * 6 + [par] * 7,
        out_specs=tok,
        out_shape=jax.ShapeDtypeStruct((m, d), BF16),
        scratch_shapes=[pltpu.VMEM((groups, MXU_DIM, MXU_DIM), F32)],
        compiler_params=_params("parallel", "parallel", "arbitrary"),
        name="wkv7",
    )(r, k, v, wl, al, g, row(w0), row(a0), row(k_k), row(k_a), row(r_k), row(gn_w), row(gn_b))


def _pad_to(x, axis, size):
    pad = [(0, 0)] * x.ndim
    pad[axis] = (0, size - x.shape[axis])
    return jnp.pad(x, pad)


def _round_up(n, mult):
    return -(-n // mult) * mult


def kernel(x, c, mod_w, mod_b, norm_g, fox_w_in, fox_b_f, fox_w_out, rwkv_mu, rwkv_w0, rwkv_w1, rwkv_w2, rwkv_a0, rwkv_a1, rwkv_a2, rwkv_g1, rwkv_g2, rwkv_k_k, rwkv_k_a, rwkv_r_k, rwkv_w_r, rwkv_w_k, rwkv_w_v, rwkv_w_o, rwkv_gn_w, rwkv_gn_b, ffn_w_in, ffn_conv_w, ffn_conv_b, ffn_w_out):
    batch, seq, d = x.shape
    depth = mod_w.shape[0]
    heads = fox_b_f.shape[1]
    bf = lambda t: t.astype(BF16)

    mod = _modulation(c, mod_w, mod_b)
    x2 = x.reshape(batch * seq, d)
    ffn_w_out_bf = bf(ffn_w_out)

    h = None
    for i in range(depth):
        sh_m, sc_m, g_m, sh_f, sc_f, g_f = [mod[i, :, n] for n in range(6)]
        j = i // 2
        ffn_pre = (norm_g[i, 2], sc_f, sh_f)
        if i % 2 == 0:
            if h is None:
                h = _prenorm(x2, norm_g[i, 0], sc_m, sh_m, seq)
            qkv = _matmul(h, fox_w_in, j, 3 * d, BF16, name="fox_qkv")
            wf = _pad_to(bf(fox_w_in[j, :, 3 * d:]), 1, LANES)
            bfp = _pad_to(fox_b_f[j].reshape(1, heads), 1, LANES)
            cum = _fox_gate(h, wf, bfp, batch, seq)
            o = _fox_attention(qkv, cum, batch, seq, heads)
            x2, h = _matmul_postnorm(o, fox_w_out, j, x2, norm_g[i, 1], g_m, seq, nxt=ffn_pre, tm=256)
        else:
            xr, xw, xk, xv, xa, xg = _rwkv_pre(x2, norm_g[i, 0], sc_m, sh_m, rwkv_mu[j], seq)
            r = _matmul(xr, rwkv_w_r, j, d, F32, name="rwkv_r")
            k = _matmul(xk, rwkv_w_k, j, d, F32, name="rwkv_k")
            v = _matmul(xv, rwkv_w_v, j, d, F32, name="rwkv_v")
            lr = _round_up(rwkv_w1.shape[2], LANES)
            wl = _lora(xw, _pad_to(bf(rwkv_w1[j]), 1, lr), _pad_to(bf(rwkv_w2[j]), 0, lr), "tanh")
            al = _lora(xa, _pad_to(bf(rwkv_a1[j]), 1, lr), _pad_to(bf(rwkv_a2[j]), 0, lr), "none")
            gr = _round_up(rwkv_g1.shape[2], LANES)
            g = _lora(xg, _pad_to(bf(rwkv_g1[j]), 1, gr), _pad_to(bf(rwkv_g2[j]), 0, gr), "sigmoid")
            o = _wkv(r, k, v, wl, al, g, rwkv_w0[j], rwkv_a0[j], rwkv_k_k[j], rwkv_k_a[j],
                     rwkv_r_k[j], rwkv_gn_w[j], rwkv_gn_b[j], batch, seq)
            x2, h = _matmul_postnorm(o, rwkv_w_o, j, x2, norm_g[i, 1], g_m, seq, nxt=ffn_pre, tm=256)
        act = _ffn_in(h, ffn_w_in, ffn_conv_w, ffn_conv_b.reshape(depth, 1, -1), i, seq)
        f = act.shape[1]
        nxt = None
        if i + 1 < depth and (i + 1) % 2 == 0:
            nxt = (norm_g[i + 1, 0], mod[i + 1, :, 1], mod[i + 1, :, 0])
        x2, h = _matmul_postnorm(act, ffn_w_out_bf, i, x2, norm_g[i, 3], g_f, seq, nxt=nxt, tk=f // 2, sub=512)
    return x2.reshape(batch, seq, d)
```

```python
import functools
import math

import jax
import jax.numpy as jnp
from jax import lax
from jax.experimental import pallas as pl
from jax.experimental.pallas import tpu as pltpu

F32 = jnp.float32
BF16 = jnp.bfloat16

LOG2E = math.log2(math.e)
NORM_EPS = 1e-6
GN_EPS = 64e-5
FOX_HEADS = 16
RWKV_HEAD_DIM = 64
CONV_WIDTH = 3

LANES = 128
SUBLANES = 8
MXU_DIM = 256
WKV_CHUNK = 64
WKV_HEADS_PER_GROUP = MXU_DIM // RWKV_HEAD_DIM
VMEM_LIMIT = 58 << 20


def _params(*sem):
    return pltpu.CompilerParams(dimension_semantics=sem, vmem_limit_bytes=VMEM_LIMIT)


def _inv_rms(x):
    return lax.rsqrt(jnp.mean(x * x, axis=-1, keepdims=True) + NORM_EPS)


def _rms(x, g):
    return x * _inv_rms(x) * g


def _dot(a, b):
    return jnp.dot(a, b, preferred_element_type=F32)


def _dot_nt(a, b):
    return lax.dot_general(a, b, (((1,), (1,)), ((), ())), preferred_element_type=F32)


def _dot_tn(a, b):
    return lax.dot_general(a, b, (((0,), (0,)), ((), ())), preferred_element_type=F32)


def _split2(x):
    hi = x.astype(BF16)
    lo = (x - hi.astype(F32)).astype(BF16)
    return hi, lo


def _split3(x):
    hi = x.astype(BF16)
    r1 = x - hi.astype(F32)
    mid = r1.astype(BF16)
    lo = (r1 - mid.astype(F32)).astype(BF16)
    return hi, mid, lo


def _softplus(x):
    return jnp.maximum(x, 0.0) + jnp.log1p(jnp.exp(-jnp.abs(x)))


def _mod_kernel(c_ref, w_ref, b_ref, o_ref):
    c = c_ref[...]
    s = c * jax.nn.sigmoid(c)
    o_ref[0] = _dot(s.astype(BF16), w_ref[0].astype(BF16)) + b_ref[0]


def _modulation(c, mod_w, mod_b):
    depth, d, n = mod_w.shape
    b = c.shape[0]
    rows = -(-b // SUBLANES) * SUBLANES
    c_pad = jnp.pad(c, ((0, rows - b), (0, 0)))
    tn = 1024
    out = pl.pallas_call(
        _mod_kernel,
        grid=(depth, n // tn),
        in_specs=[pl.BlockSpec((rows, d), lambda l, j: (0, 0)),
                  pl.BlockSpec((1, d, tn), lambda l, j: (l, 0, j)),
                  pl.BlockSpec((1, 1, tn), lambda l, j: (l, 0, j))],
        out_specs=pl.BlockSpec((1, rows, tn), lambda l, j: (l, 0, j)),
        out_shape=jax.ShapeDtypeStruct((depth, rows, n), F32),
        compiler_params=_params("parallel", "parallel"),
        name="adaln_mod",
    )(c_pad, mod_w, mod_b.reshape(depth, 1, n))
    return out[:, :b].reshape(depth, b, 6, 1, d)


def _prenorm_kernel(x_ref, g_ref, sc_ref, sh_ref, o_ref):
    inv = _inv_rms(x_ref[...])
    for c in range(x_ref.shape[1] // MXU_DIM):
        cols = slice(c * MXU_DIM, (c + 1) * MXU_DIM)
        h = x_ref[:, cols] * inv * g_ref[:, cols] * (1.0 + sc_ref[0, :, cols]) + sh_ref[0, :, cols]
        o_ref[:, cols] = h.astype(o_ref.dtype)


def _prenorm(x2, g, sc, sh, seq, tm=512):
    m, d = x2.shape
    tpb = seq // tm
    return pl.pallas_call(
        _prenorm_kernel,
        grid=(m // tm,),
        in_specs=[pl.BlockSpec((tm, d), lambda i: (i, 0)),
                  pl.BlockSpec((1, d), lambda i: (0, 0)),
                  pl.BlockSpec((1, 1, d), lambda i: (i // tpb, 0, 0)),
                  pl.BlockSpec((1, 1, d), lambda i: (i // tpb, 0, 0))],
        out_specs=pl.BlockSpec((tm, d), lambda i: (i, 0)),
        out_shape=jax.ShapeDtypeStruct((m, d), BF16),
        compiler_params=_params("parallel"),
        name="prenorm",
    )(x2, g.reshape(1, d), sc, sh)


def _mm_kernel(a_ref, w_ref, o_ref, wb_ref, *, w_is_transposed):
    @pl.when(pl.program_id(1) == 0)
    def _():
        wb_ref[...] = w_ref[...].astype(BF16)

    dot = _dot_nt if w_is_transposed else _dot
    o_ref[...] = dot(a_ref[...], wb_ref[...]).astype(o_ref.dtype)


def _matmul(a, w, layer, n, out_dtype, tm=2048, tn=512, w_is_transposed=False, name="matmul"):
    m, k = a.shape
    if w_is_transposed:
        w_spec, w_tile = pl.BlockSpec((None, tn, k), lambda j, i: (layer, j, 0)), (tn, k)
    else:
        w_spec, w_tile = pl.BlockSpec((None, k, tn), lambda j, i: (layer, 0, j)), (k, tn)
    return pl.pallas_call(
        functools.partial(_mm_kernel, w_is_transposed=w_is_transposed),
        grid=(n // tn, m // tm),
        in_specs=[pl.BlockSpec((tm, k), lambda j, i: (i, 0)), w_spec],
        out_specs=pl.BlockSpec((tm, tn), lambda j, i: (i, j)),
        out_shape=jax.ShapeDtypeStruct((m, n), out_dtype),
        scratch_shapes=[pltpu.VMEM(w_tile, BF16)],
        compiler_params=_params("parallel", "arbitrary"),
        name=name,
    )(a, w)


def _out_kernel(*refs, nk, with_h, cast_w, sub):
    a_ref, w_ref, x_ref, g1_ref, gate_ref = refs[:5]
    if with_h:
        g2_ref, sc_ref, sh_ref, xo_ref, ho_ref = refs[5:10]
    else:
        xo_ref = refs[5]
    acc_ref = xo_ref
    if cast_w:
        wb_ref = refs[-1]

        @pl.when(pl.program_id(0) == 0)
        def _():
            wb_ref[...] = w_ref[...].astype(BF16)

        w_ref = wb_ref

    def epilogue(rows, y):
        xn = x_ref[rows, :] + gate_ref[0] * _rms(y, g1_ref[...])
        xo_ref[rows, :] = xn
        if with_h:
            h = _rms(xn, g2_ref[...]) * (1.0 + sc_ref[0]) + sh_ref[0]
            ho_ref[rows, :] = h.astype(ho_ref.dtype)

    def finish():
        tm = a_ref.shape[0]
        tiles = [slice(r, r + sub) for r in range(0, tm, sub)]

        def total(rows):
            part = _dot(a_ref[rows, :], w_ref[...])
            return part if nk == 1 else acc_ref[rows, :] + part

        y = total(tiles[0])
        for n, rows in enumerate(tiles):
            y_next = total(tiles[n + 1]) if n + 1 < len(tiles) else None
            epilogue(rows, y)
            y = y_next

    if nk == 1:
        finish()
        return
    k = pl.program_id(1)

    def accumulate(first):
        for r in range(0, a_ref.shape[0], sub):
            part = _dot(a_ref[r:r + sub, :], w_ref[...])
            acc_ref[r:r + sub, :] = part if first else acc_ref[r:r + sub, :] + part

    pl.when(k == 0)(functools.partial(accumulate, True))
    pl.when(jnp.logical_and(k > 0, k < nk - 1))(functools.partial(accumulate, False))
    pl.when(k == nk - 1)(finish)


def _matmul_postnorm(a, w, layer, x2, g1, gate, seq, nxt=None, tm=512, tk=None, sub=128):
    m, kdim = a.shape
    d = w.shape[-1]
    cast_w = w.dtype != BF16
    tk = kdim if tk is None else tk
    nk = kdim // tk
    assert not (cast_w and nk > 1)
    tpb = seq // tm
    with_h = nxt is not None
    row = lambda i, k: (i, 0)
    per_batch = lambda i, k: (i // tpb, 0, 0)
    const = lambda i, k: (0, 0)
    if cast_w:
        w_spec = pl.BlockSpec((None, kdim, d), lambda i, k: (layer, 0, 0), pipeline_mode=pl.Buffered(1))
    else:
        w_spec = pl.BlockSpec((None, tk, d), lambda i, k: (layer, k, 0))
    in_specs = [pl.BlockSpec((tm, tk), lambda i, k: (i, k)),
                w_spec,
                pl.BlockSpec((tm, d), row),
                pl.BlockSpec((1, d), const),
                pl.BlockSpec((1, 1, d), per_batch)]
    args = [a, w, x2, g1.reshape(1, d), gate]
    out_specs = [pl.BlockSpec((tm, d), row)]
    out_shape = [jax.ShapeDtypeStruct((m, d), F32)]
    if with_h:
        g2, sc, sh = nxt
        in_specs += [pl.BlockSpec((1, d), const), pl.BlockSpec((1, 1, d), per_batch),
                     pl.BlockSpec((1, 1, d), per_batch)]
        args += [g2.reshape(1, d), sc, sh]
        out_specs.append(pl.BlockSpec((tm, d), row))
        out_shape.append(jax.ShapeDtypeStruct((m, d), BF16))
    res = pl.pallas_call(
        functools.partial(_out_kernel, nk=nk, with_h=with_h, cast_w=cast_w, sub=sub),
        grid=(m // tm, nk),
        in_specs=in_specs,
        out_specs=out_specs,
        out_shape=out_shape,
        scratch_shapes=[pltpu.VMEM((kdim, d), BF16)] if cast_w else [],
        compiler_params=_params("arbitrary", "arbitrary"),
        name="matmul_postnorm",
    )(*args)
    return (res[0], res[1]) if with_h else (res[0], None)


def _fox_gate_kernel(h_ref, wf_ref, bf_ref, o_ref, carry_ref, *, ts):
    @pl.when(pl.program_id(1) == 0)
    def _():
        carry_ref[...] = jnp.zeros_like(carry_ref)

    logit = _dot(h_ref[...], wf_ref[...]) + bf_ref[...]
    log_f = -_softplus(-logit)
    row = lax.broadcasted_iota(jnp.int32, (ts, ts), 0)
    col = lax.broadcasted_iota(jnp.int32, (ts, ts), 1)
    tri = jnp.where(row >= col, 1.0, 0.0).astype(BF16)
    hi, mid, lo = _split3(log_f)
    cs = _dot(tri, hi) + _dot(tri, mid) + _dot(tri, lo) + carry_ref[...]
    o_ref[...] = cs
    carry_ref[...] = cs[ts - 1:ts, :]


def _fox_gate(h, wf_pad, bf_pad, batch, seq, ts=256):
    m, d = h.shape
    spb = seq // ts
    return pl.pallas_call(
        functools.partial(_fox_gate_kernel, ts=ts),
        grid=(batch, spb),
        in_specs=[pl.BlockSpec((ts, d), lambda b, s: (b * spb + s, 0)),
                  pl.BlockSpec((d, LANES), lambda b, s: (0, 0)),
                  pl.BlockSpec((1, LANES), lambda b, s: (0, 0))],
        out_specs=pl.BlockSpec((ts, LANES), lambda b, s: (b * spb + s, 0)),
        out_shape=jax.ShapeDtypeStruct((m, LANES), F32),
        scratch_shapes=[pltpu.VMEM((1, LANES), F32)],
        compiler_params=_params("parallel", "arbitrary"),
        name="fox_gate",
    )(h, wf_pad, bf_pad)


def _fox_attn_kernel(q_ref, k_ref, v_ref, cq_ref, ck_ref, o_ref, *, tq, tk, scale):
    head = pl.program_id(1)
    i = pl.program_id(2)
    ratio = tq // tk
    hd = q_ref.shape[1]
    lane = lax.broadcasted_iota(jnp.int32, (tq, LANES), 1)
    cq = jnp.sum(jnp.where(lane == head, cq_ref[...], 0.0), axis=1, keepdims=True) * LOG2E
    row_id = lax.broadcasted_iota(jnp.int32, (tk, tk), 0)
    col_id = lax.broadcasted_iota(jnp.int32, (tk, tk), 1)

    def update(j, row0, masked, carry):
        start = pl.multiple_of(j * tk, tk)
        kj = k_ref[pl.ds(start, tk), :]
        vj = v_ref[pl.ds(start, tk), :]
        ckj = ck_ref[0, 0, j] * LOG2E
        m_prev, l_prev, acc = carry
        s = _dot_nt(q_ref[row0:, :], kj) * (scale * LOG2E) + cq[row0:] - ckj
        if masked:
            top = jnp.where(row_id >= col_id, s[:tk], -jnp.inf)
            s = top if tq - row0 == tk else jnp.concatenate([top, s[tk:]], axis=0)
        m_new = jnp.maximum(m_prev, jnp.max(s, axis=1, keepdims=True))
        alpha = jnp.exp2(m_prev - m_new)
        p = jnp.exp2(s - m_new)
        l_new = alpha * l_prev + jnp.sum(p, axis=1, keepdims=True)
        return m_new, l_new, alpha * acc + _dot(p.astype(BF16), vj)

    init = (jnp.full((tq, 1), -jnp.inf, F32), jnp.zeros((tq, 1), F32), jnp.zeros((tq, hd), F32))
    carry = lax.fori_loop(0, ratio * i, lambda j, c: update(j, 0, False, c), init)
    done = []
    for t in range(ratio):
        carry = update(ratio * i + t, t * tk, True, carry)
        done.append(tuple(x[:tk] for x in carry))
        if t + 1 < ratio:
            carry = tuple(x[tk:] for x in carry)
    l_fin = jnp.concatenate([d[1] for d in done], axis=0)
    acc = jnp.concatenate([d[2] for d in done], axis=0)
    o_ref[...] = (acc / l_fin).astype(o_ref.dtype)


def _fox_attention(qkv, cum, batch, seq, heads, tq=2048, tk=256):
    m = qkv.shape[0]
    hd = qkv.shape[1] // (3 * heads)
    nq = seq // tq
    nk = seq // tk
    ck = cum[:, :heads].reshape(batch, seq, heads).transpose(0, 2, 1).reshape(batch, heads, nk, 1, tk)
    return pl.pallas_call(
        functools.partial(_fox_attn_kernel, tq=tq, tk=tk, scale=hd ** -0.5),
        grid=(batch, heads, nq),
        in_specs=[pl.BlockSpec((tq, hd), lambda b, h, i: (b * nq + i, h)),
                  pl.BlockSpec((seq, hd), lambda b, h, i: (b, heads + h)),
                  pl.BlockSpec((seq, hd), lambda b, h, i: (b, 2 * heads + h)),
                  pl.BlockSpec((tq, LANES), lambda b, h, i: (b * nq + i, 0)),
                  pl.BlockSpec((1, 1, nk, 1, tk), lambda b, h, i: (b, h, 0, 0, 0))],
        out_specs=pl.BlockSpec((tq, hd), lambda b, h, i: (b * nq + i, h)),
        out_shape=jax.ShapeDtypeStruct((m, heads * hd), BF16),
        compiler_params=_params("parallel", "parallel", "arbitrary"),
        name="fox_attention",
    )(qkv, qkv, qkv, cum, ck)


def _ffn_in_kernel(h_ref, w1_ref, w2_ref, cw1_ref, cw2_ref, cb1_ref, cb2_ref, o_ref, tail_ref, wb_ref,
                   *, tm, sub, tpb):
    @pl.when(pl.program_id(1) % tpb == 0)
    def _():
        tail_ref[...] = jnp.zeros_like(tail_ref)

    @pl.when(pl.program_id(1) == 0)
    def _():
        wb_ref[0] = w1_ref[...].astype(BF16)
        wb_ref[1] = w2_ref[...].astype(BF16)

    row = lax.broadcasted_iota(jnp.int32, (SUBLANES, 1), 0)
    w_refs = (wb_ref.at[0], wb_ref.at[1])
    cw = (cw1_ref[...], cw2_ref[...])
    cb = (cb1_ref[...], cb2_ref[...])

    def up(r):
        rows = h_ref[r * sub:(r + 1) * sub, :]
        return [_dot(rows, w_ref[...]) for w_ref in w_refs]

    def shifted(u, tail, shift):
        rolled = pltpu.roll(u, shift, 0)
        head = rolled[:SUBLANES]
        for s in range(shift):
            src = SUBLANES - shift + s
            head = jnp.where(row == s, tail[src:src + 1], head)
        return jnp.concatenate([head, rolled[SUBLANES:]], axis=0)

    def conv(n, u, tail):
        return cw[n][0:1] * shifted(u, tail, 2) + cw[n][1:2] * shifted(u, tail, 1) + cw[n][2:3] * u + cb[n]

    nsub = tm // sub
    tails = [tail_ref[0], tail_ref[1]]
    u = up(0)
    for r in range(nsub):
        u_next = up(r + 1) if r + 1 < nsub else None
        x1 = conv(0, u[0], tails[0])
        x2 = conv(1, u[1], tails[1])
        gelu = 0.5 * x1 * (1.0 + lax.erf(x1 * (2.0 ** -0.5)))
        o_ref[r * sub:(r + 1) * sub, :] = (gelu * x2).astype(o_ref.dtype)
        tails = [x[sub - SUBLANES:] for x in u]
        u = u_next
    tail_ref[0] = tails[0]
    tail_ref[1] = tails[1]


def _ffn_in(h, w_in, conv_w, conv_b, layer, seq, tm=2048, tn=512, sub=256):
    m, d = h.shape
    f = w_in.shape[2] // 2
    nf = f // tn
    tpb = seq // tm
    return pl.pallas_call(
        functools.partial(_ffn_in_kernel, tm=tm, sub=sub, tpb=tpb),
        grid=(nf, m // tm),
        in_specs=[pl.BlockSpec((tm, d), lambda j, i: (i, 0)),
                  pl.BlockSpec((None, d, tn), lambda j, i: (layer, 0, j)),
                  pl.BlockSpec((None, d, tn), lambda j, i: (layer, 0, nf + j)),
                  pl.BlockSpec((None, CONV_WIDTH, tn), lambda j, i: (layer, 0, j)),
                  pl.BlockSpec((None, CONV_WIDTH, tn), lambda j, i: (layer, 0, nf + j)),
                  pl.BlockSpec((None, 1, tn), lambda j, i: (layer, 0, j)),
                  pl.BlockSpec((None, 1, tn), lambda j, i: (layer, 0, nf + j))],
        out_specs=pl.BlockSpec((tm, tn), lambda j, i: (i, j)),
        out_shape=jax.ShapeDtypeStruct((m, f), BF16),
        scratch_shapes=[pltpu.VMEM((2, SUBLANES, tn), F32), pltpu.VMEM((2, d, tn), BF16)],
        compiler_params=_params("parallel", "arbitrary"),
        name="ffn_in",
    )(h, w_in, w_in, conv_w, conv_w, conv_b, conv_b)


def _rwkv_pre_kernel(x_ref, halo_ref, g_ref, sc_ref, sh_ref, mu_ref, *o_refs, tm, tpb):
    first = pl.program_id(0) % tpb == 0
    inv = _inv_rms(x_ref[...])
    inv_halo = _inv_rms(halo_ref[...])
    row = lax.broadcasted_iota(jnp.int32, (SUBLANES, 1), 0)
    for c in range(x_ref.shape[1] // MXU_DIM):
        cols = slice(c * MXU_DIM, (c + 1) * MXU_DIM)
        mod = lambda t, s: t * s * g_ref[:, cols] * (1.0 + sc_ref[0, :, cols]) + sh_ref[0, :, cols]
        h = mod(x_ref[:, cols], inv)
        h_halo = jnp.where(first, 0.0, mod(halo_ref[:, cols], inv_halo))
        rolled = pltpu.roll(h, 1, 0)
        head = jnp.where(row == 0, h_halo[SUBLANES - 1:SUBLANES], rolled[:SUBLANES])
        xx = jnp.concatenate([head, rolled[SUBLANES:]], axis=0) - h
        for n, o_ref in enumerate(o_refs):
            o_ref[:, cols] = (h + xx * mu_ref[n:n + 1, cols]).astype(o_ref.dtype)


def _rwkv_pre(x2, g, sc, sh, mu, seq, tm=256):
    m, d = x2.shape
    tpb = seq // tm
    hb = tm // SUBLANES
    nmix = mu.shape[0]
    return pl.pallas_call(
        functools.partial(_rwkv_pre_kernel, tm=tm, tpb=tpb),
        grid=(m // tm,),
        in_specs=[pl.BlockSpec((tm, d), lambda i: (i, 0)),
                  pl.BlockSpec((SUBLANES, d), lambda i: (jnp.maximum(i * hb - 1, 0), 0)),
                  pl.BlockSpec((1, d), lambda i: (0, 0)),
                  pl.BlockSpec((1, 1, d), lambda i: (i // tpb, 0, 0)),
                  pl.BlockSpec((1, 1, d), lambda i: (i // tpb, 0, 0)),
                  pl.BlockSpec((nmix, d), lambda i: (0, 0))],
        out_specs=[pl.BlockSpec((tm, d), lambda i: (i, 0))] * nmix,
        out_shape=[jax.ShapeDtypeStruct((m, d), BF16)] * nmix,
        compiler_params=_params("parallel"),
        name="rwkv_pre",
    )(x2, x2, g.reshape(1, d), sc, sh, mu)


def _lora_kernel(x_ref, w1_ref, w2_ref, o_ref, *, act):
    t = _dot(x_ref[...], w1_ref[...])
    if act == "tanh":
        t = jnp.tanh(t)
    elif act == "sigmoid":
        t = jax.nn.sigmoid(t)
    o_ref[...] = _dot(t.astype(BF16), w2_ref[...])


def _lora(x, w1, w2, act, tm=512):
    m, d = x.shape
    r = w1.shape[1]
    n = w2.shape[1]
    return pl.pallas_call(
        functools.partial(_lora_kernel, act=act),
        grid=(m // tm,),
        in_specs=[pl.BlockSpec((tm, d), lambda i: (i, 0)),
                  pl.BlockSpec((d, r), lambda i: (0, 0)),
                  pl.BlockSpec((r, n), lambda i: (0, 0))],
        out_specs=pl.BlockSpec((tm, n), lambda i: (i, 0)),
        out_shape=jax.ShapeDtypeStruct((m, n), F32),
        compiler_params=_params("parallel"),
        name="rwkv_lora_" + act,
    )(x, w1, w2)


def _wkv_kernel(r_ref, k_ref, v_ref, wl_ref, al_ref, g_ref,
                w0_ref, a0_ref, kk_ref, ka_ref, rk_ref, gnw_ref, gnb_ref,
                o_ref, s_ref, *, groups, chunks):
    L = WKV_CHUNK
    N = RWKV_HEAD_DIM
    W = MXU_DIM

    @pl.when(pl.program_id(2) == 0)
    def _():
        s_ref[...] = jnp.zeros_like(s_ref)

    ri = lax.broadcasted_iota(jnp.int32, (W, W), 0)
    ci = lax.broadcasted_iota(jnp.int32, (W, W), 1)
    bd = (ri // N) == (ci // N)
    bd_bf = jnp.where(bd, 1.0, 0.0).astype(BF16)
    t_i = lax.broadcasted_iota(jnp.int32, (L, W), 0)
    s_i = lax.broadcasted_iota(jnp.int32, (L, W), 1) % N
    strict = t_i > s_i
    incl = t_i >= s_i
    eye = jnp.where(t_i == s_i, 1.0, 0.0)
    tri = jnp.where(lax.broadcasted_iota(jnp.int32, (L, L), 0) >= lax.broadcasted_iota(jnp.int32, (L, L), 1),
                    1.0, 0.0).astype(BF16)

    def block_diag(x):
        xb = x.astype(BF16)
        return jnp.concatenate([xb] * (W // L), axis=0) * bd_bf

    def stack(a, b):
        return jnp.concatenate([a, b], axis=0).astype(BF16)

    def head_sums(xs):
        res = _dot(jnp.concatenate(xs, axis=0).astype(BF16), bd_bf)
        return [res[n * L:(n + 1) * L] for n in range(len(xs))]

    def each(fn, *cols):
        return [fn(*args) for args in zip(*cols)]

    def cumsum(x):
        hi, mid, lo = _split3(x)
        return _dot(tri, hi) + _dot(tri, mid) + _dot(tri, lo)

    lanes = [slice(gi * W, (gi + 1) * W) for gi in range(groups)]

    def par(ref):
        return [ref[:, sl] for sl in lanes]

    def prepare(c):
        rows = slice(c * L, (c + 1) * L)
        tok = lambda ref: [ref[rows, sl] for sl in lanes]
        r, v, k_raw = tok(r_ref), tok(v_ref), tok(k_ref)
        lw = each(lambda w0, wl: -math.exp(-0.5) * jax.nn.sigmoid(w0 + wl), par(w0_ref), tok(wl_ref))
        a = each(lambda a0, al: jax.nn.sigmoid(a0 + al), par(a0_ref), tok(al_ref))
        kk = each(lambda kx, kkp: kx * kkp, k_raw, par(kk_ref))
        k = each(lambda kx, ax, ka: kx * (1.0 + (ax - 1.0) * ka), k_raw, a, par(ka_ref))
        kk = each(lambda x, sq: x / jnp.maximum(jnp.sqrt(sq), 1e-12), kk, head_sums([x * x for x in kk]))
        cum = each(cumsum, lw)
        cum_last = [cx[L - 1:L] for cx in cum]
        p_inv = each(lambda cx: jnp.exp(-cx), cum)
        decay = [jnp.exp(cl) for cl in cum_last]
        p_rest = each(lambda pi, dx: pi * dx, p_inv, decay)
        b = each(lambda x, ax: x * ax, kk, a)
        lhs_ar = each(lambda x, cx, lx, rx: stack(-x * jnp.exp(cx - lx), rx * jnp.exp(cx)), kk, cum, lw, r)
        rkr = each(lambda rx, kx, rk: rx * kx * rk, r, k, par(rk_ref))
        bonus = each(lambda sx, vx: sx * vx, head_sums(rkr), v)
        kb_rest =each(lambda bx, kx, pr: stack(bx * pr, kx * pr), b, k, p_rest)
        b_w = each(lambda bx, pi: block_diag(bx * pi), b, p_inv)
        k_w = each(lambda kx, pi: block_diag(kx * pi), k, p_inv)
        ab = each(_dot_nt, lhs_ar, b_w)
        a_ab = [jnp.where(strict, x[:L], 0.0) for x in ab]
        a_rb = [jnp.where(incl, x[L:], 0.0) for x in ab]
        inv = [eye + x for x in a_ab]
        npow = [x.astype(BF16) for x in a_ab]
        npow = each(lambda n: _dot(n, block_diag(n)).astype(BF16), npow)
        for _ in range(4):
            res = each(lambda n, t: _dot(jnp.concatenate([n, t.astype(BF16)], axis=0), block_diag(n)), npow, inv)
            npow = [x[:L].astype(BF16) for x in res]
            inv = [t + x[L:] for t, x in zip(inv, res)]
        inv = each(lambda t, n: t + _dot(t.astype(BF16), block_diag(n)), inv, npow)
        ak = each(_dot_nt, lhs_ar, k_w)
        a_ak = [jnp.where(strict, x[:L], 0.0) for x in ak]
        a_rk = [jnp.where(incl, x[L:], 0.0) for x in ak]
        a_v = each(lambda x, y, vx: _dot(stack(x, y), block_diag(vx)), a_ak, a_rk, v)
        return dict(rows=rows, v=v, lhs_ar=lhs_ar, inv=inv, a_v=a_v, a_rb=a_rb, bonus=bonus,
                    decay=decay, kb_rest=kb_rest, gate=tok(g_ref))

    prepared = [prepare(c) for c in range(chunks)]
    gn_w, gn_b = par(gnw_ref), par(gnb_ref)

    state = [s_ref[gi] for gi in range(groups)]
    for q in prepared:
        a_s = each(lambda lhs, s: _dot_nt(lhs, s.astype(BF16)), q["lhs_ar"], state)
        z = each(lambda t, av, as_: _dot(t.astype(BF16), block_diag(av[:L] + as_[:L])), q["inv"], q["a_v"], a_s)
        y = each(lambda as_, av, arb, zx: as_[L:] + av[L:] + _dot(arb.astype(BF16), block_diag(zx)),
                 a_s, q["a_v"], q["a_rb"], z)
        upd = each(lambda zx, vx, kb: _dot_tn(stack(zx, vx), kb), z, q["v"], q["kb_rest"])
        state = each(lambda s, dx, ux: jnp.where(bd, s * dx + ux, 0.0), state, q["decay"], upd)

        mean = [m * (1.0 / N) for m in head_sums(y)]
        yc = each(lambda yx, mx: yx - mx, y, mean)
        var = [m * (1.0 / N) for m in head_sums([x * x for x in yc])]
        for gi, sl in enumerate(lanes):
            yn = yc[gi] * lax.rsqrt(var[gi] + GN_EPS) * gn_w[gi] + gn_b[gi]
            o_ref[q["rows"], sl] = ((yn + q["bonus"][gi]) * q["gate"][gi]).astype(o_ref.dtype)
    for gi in range(groups):
        s_ref[gi] = state[gi]


def _wkv(r, k, v, wl, al, g, w0, a0, k_k, k_a, r_k, gn_w, gn_b, batch, seq, groups=8, chunks=2):
    m, d = r.shape
    rows = chunks * WKV_CHUNK
    gw = groups * MXU_DIM
    ng = d // gw
    nc = seq // rows
    tok = pl.BlockSpec((rows, gw), lambda b, gidx, c: (b * nc + c, gidx))
    par = pl.BlockSpec((1, gw), lambda b, gidx, c: (0, gidx))
    row = lambda t: t.reshape(1, d)
    return pl.pallas_call(
        functools.partial(_wkv_kernel, groups=groups, chunks=chunks),
        grid=(batch, ng, nc),
        in_specs=[tok] * 6 + [par] * 7,
        out_specs=tok,
        out_shape=jax.ShapeDtypeStruct((m, d), BF16),
        scratch_shapes=[pltpu.VMEM((groups, MXU_DIM, MXU_DIM), F32)],
        compiler_params=_params("parallel", "parallel", "arbitrary"),
        name="wkv7",
    )(r, k, v, wl, al, g, row(w0), row(a0), row(k_k), row(k_a), row(r_k), row(gn_w), row(gn_b))


def _pad_to(x, axis, size):
    pad = [(0, 0)] * x.ndim
    pad[axis] = (0, size - x.shape[axis])
    return jnp.pad(x, pad)


def _round_up(n, mult):
    return -(-n // mult) * mult


def kernel(x, c, mod_w, mod_b, norm_g, fox_w_in, fox_b_f, fox_w_out, rwkv_mu, rwkv_w0, rwkv_w1, rwkv_w2, rwkv_a0, rwkv_a1, rwkv_a2, rwkv_g1, rwkv_g2, rwkv_k_k, rwkv_k_a, rwkv_r_k, rwkv_w_r, rwkv_w_k, rwkv_w_v, rwkv_w_o, rwkv_gn_w, rwkv_gn_b, ffn_w_in, ffn_conv_w, ffn_conv_b, ffn_w_out):
    batch, seq, d = x.shape
    depth = mod_w.shape[0]
    heads = fox_b_f.shape[1]
    bf = lambda t: t.astype(BF16)

    mod = _modulation(c, mod_w, mod_b)
    x2 = x.reshape(batch * seq, d)
    ffn_w_out_bf = bf(ffn_w_out)

    h = None
    for i in range(depth):
        sh_m, sc_m, g_m, sh_f, sc_f, g_f = [mod[i, :, n] for n in range(6)]
        j = i // 2
        ffn_pre = (norm_g[i, 2], sc_f, sh_f)
        if i % 2 == 0:
            if h is None:
                h = _prenorm(x2, norm_g[i, 0], sc_m, sh_m, seq)
            qkv = _matmul(h, jnp.swapaxes(fox_w_in, 1, 2), j, 3 * d, BF16, w_is_transposed=True, name="fox_qkv")
            wf = _pad_to(bf(fox_w_in[j, :, 3 * d:]), 1, LANES)
            bfp = _pad_to(fox_b_f[j].reshape(1, heads), 1, LANES)
            cum = _fox_gate(h, wf, bfp, batch, seq)
            o = _fox_attention(qkv, cum, batch, seq, heads)
            x2, h = _matmul_postnorm(o, fox_w_out, j, x2, norm_g[i, 1], g_m, seq, nxt=ffn_pre, sub=512)
        else:
            xr, xw, xk, xv, xa, xg = _rwkv_pre(x2, norm_g[i, 0], sc_m, sh_m, rwkv_mu[j], seq)
            r = _matmul(xr, rwkv_w_r, j, d, F32, name="rwkv_r")
            k = _matmul(xk, rwkv_w_k, j, d, F32, name="rwkv_k")
            v = _matmul(xv, rwkv_w_v, j, d, F32, name="rwkv_v")
            lr = _round_up(rwkv_w1.shape[2], LANES)
            wl = _lora(xw, _pad_to(bf(rwkv_w1[j]), 1, lr), _pad_to(bf(rwkv_w2[j]), 0, lr), "tanh")
            al = _lora(xa, _pad_to(bf(rwkv_a1[j]), 1, lr), _pad_to(bf(rwkv_a2[j]), 0, lr), "none")
            gr = _round_up(rwkv_g1.shape[2], LANES)
            g = _lora(xg, _pad_to(bf(rwkv_g1[j]), 1, gr), _pad_to(bf(rwkv_g2[j]), 0, gr), "sigmoid")
            o = _wkv(r, k, v, wl, al, g, rwkv_w0[j], rwkv_a0[j], rwkv_k_k[j], rwkv_k_a[j],
                     rwkv_r_k[j], rwkv_gn_w[j], rwkv_gn_b[j], batch, seq)
            x2, h = _matmul_postnorm(o, rwkv_w_o, j, x2, norm_g[i, 1], g_m, seq, nxt=ffn_pre, sub=512)
        act = _ffn_in(h, ffn_w_in, ffn_conv_w, ffn_conv_b.reshape(depth, 1, -1), i, seq)
        f = act.shape[1]
        nxt = None
        if i + 1 < depth and (i + 1) % 2 == 0:
            nxt = (norm_g[i + 1, 0], mod[i + 1, :, 1], mod[i + 1, :, 0])
        x2, h = _matmul_postnorm(act, ffn_w_out_bf, i, x2, norm_g[i, 3], g_f, seq, nxt=nxt, tk=f // 2, sub=512)
    return x2.reshape(batch, seq, d)
```

```python
import functools
import math

import jax
import jax.numpy as jnp
from jax import lax
from jax.experimental import pallas as pl
from jax.experimental.pallas import tpu as pltpu

F32 = jnp.float32
BF16 = jnp.bfloat16

LOG2E = math.log2(math.e)
NORM_EPS = 1e-6
GN_EPS = 64e-5
FOX_HEADS = 16
RWKV_HEAD_DIM = 64
CONV_WIDTH = 3

LANES = 128
SUBLANES = 8
MXU_DIM = 256
WKV_CHUNK = 64
WKV_HEADS_PER_GROUP = MXU_DIM // RWKV_HEAD_DIM
VMEM_LIMIT = 58 << 20


def _params(*sem):
    return pltpu.CompilerParams(dimension_semantics=sem, vmem_limit_bytes=VMEM_LIMIT)


def _inv_rms(x):
    return lax.rsqrt(jnp.mean(x * x, axis=-1, keepdims=True) + NORM_EPS)


def _rms(x, g):
    return x * _inv_rms(x) * g


def _dot(a, b):
    return jnp.dot(a, b, preferred_element_type=F32)


def _dot_nt(a, b):
    return lax.dot_general(a, b, (((1,), (1,)), ((), ())), preferred_element_type=F32)


def _dot_tn(a, b):
    return lax.dot_general(a, b, (((0,), (0,)), ((), ())), preferred_element_type=F32)


def _split2(x):
    hi = x.astype(BF16)
    lo = (x - hi.astype(F32)).astype(BF16)
    return hi, lo


def _split3(x):
    hi = x.astype(BF16)
    r1 = x - hi.astype(F32)
    mid = r1.astype(BF16)
    lo = (r1 - mid.astype(F32)).astype(BF16)
    return hi, mid, lo


def _softplus(x):
    return jnp.maximum(x, 0.0) + jnp.log1p(jnp.exp(-jnp.abs(x)))


def _mod_kernel(c_ref, w_ref, b_ref, o_ref):
    c = c_ref[...]
    s = c * jax.nn.sigmoid(c)
    o_ref[0] = _dot(s.astype(BF16), w_ref[0].astype(BF16)) + b_ref[0]


def _modulation(c, mod_w, mod_b):
    depth, d, n = mod_w.shape
    b = c.shape[0]
    rows = -(-b // SUBLANES) * SUBLANES
    c_pad = jnp.pad(c, ((0, rows - b), (0, 0)))
    tn = 1024
    out = pl.pallas_call(
        _mod_kernel,
        grid=(depth, n // tn),
        in_specs=[pl.BlockSpec((rows, d), lambda l, j: (0, 0)),
                  pl.BlockSpec((1, d, tn), lambda l, j: (l, 0, j)),
                  pl.BlockSpec((1, 1, tn), lambda l, j: (l, 0, j))],
        out_specs=pl.BlockSpec((1, rows, tn), lambda l, j: (l, 0, j)),
        out_shape=jax.ShapeDtypeStruct((depth, rows, n), F32),
        compiler_params=_params("parallel", "parallel"),
        name="adaln_mod",
    )(c_pad, mod_w, mod_b.reshape(depth, 1, n))
    return out[:, :b].reshape(depth, b, 6, 1, d)


def _prenorm_kernel(x_ref, g_ref, sc_ref, sh_ref, o_ref):
    inv = _inv_rms(x_ref[...])
    for c in range(x_ref.shape[1] // MXU_DIM):
        cols = slice(c * MXU_DIM, (c + 1) * MXU_DIM)
        h = x_ref[:, cols] * inv * g_ref[:, cols] * (1.0 + sc_ref[0, :, cols]) + sh_ref[0, :, cols]
        o_ref[:, cols] = h.astype(o_ref.dtype)


def _prenorm(x2, g, sc, sh, seq, tm=512):
    m, d = x2.shape
    tpb = seq // tm
    return pl.pallas_call(
        _prenorm_kernel,
        grid=(m // tm,),
        in_specs=[pl.BlockSpec((tm, d), lambda i: (i, 0)),
                  pl.BlockSpec((1, d), lambda i: (0, 0)),
                  pl.BlockSpec((1, 1, d), lambda i: (i // tpb, 0, 0)),
                  pl.BlockSpec((1, 1, d), lambda i: (i // tpb, 0, 0))],
        out_specs=pl.BlockSpec((tm, d), lambda i: (i, 0)),
        out_shape=jax.ShapeDtypeStruct((m, d), BF16),
        compiler_params=_params("parallel"),
        name="prenorm",
    )(x2, g.reshape(1, d), sc, sh)


def _mm_kernel(a_ref, w_ref, o_ref, wb_ref, *, w_is_transposed):
    @pl.when(pl.program_id(1) == 0)
    def _():
        wb_ref[...] = w_ref[...].astype(BF16)

    dot = _dot_nt if w_is_transposed else _dot
    o_ref[...] = dot(a_ref[...], wb_ref[...]).astype(o_ref.dtype)


def _matmul(a, w, layer, n, out_dtype, tm=2048, tn=512, w_is_transposed=False, name="matmul"):
    m, k = a.shape
    if w_is_transposed:
        w_spec, w_tile = pl.BlockSpec((None, tn, k), lambda j, i: (layer, j, 0)), (tn, k)
    else:
        w_spec, w_tile = pl.BlockSpec((None, k, tn), lambda j, i: (layer, 0, j)), (k, tn)
    return pl.pallas_call(
        functools.partial(_mm_kernel, w_is_transposed=w_is_transposed),
        grid=(n // tn, m // tm),
        in_specs=[pl.BlockSpec((tm, k), lambda j, i: (i, 0)), w_spec],
        out_specs=pl.BlockSpec((tm, tn), lambda j, i: (i, j)),
        out_shape=jax.ShapeDtypeStruct((m, n), out_dtype),
        scratch_shapes=[pltpu.VMEM(w_tile, BF16)],
        compiler_params=_params("parallel", "arbitrary"),
        name=name,
    )(a, w)


def _out_kernel(*refs, nk, with_h, cast_w, sub):
    a_ref, w_ref, x_ref, g1_ref, gate_ref = refs[:5]
    if with_h:
        g2_ref, sc_ref, sh_ref, xo_ref, ho_ref = refs[5:10]
    else:
        xo_ref = refs[5]
    acc_ref = xo_ref
    if cast_w:
        wb_ref = refs[-1]

        @pl.when(pl.program_id(0) == 0)
        def _():
            wb_ref[...] = w_ref[...].astype(BF16)

        w_ref = wb_ref

    def epilogue(rows, y):
        xn = x_ref[rows, :] + gate_ref[0] * _rms(y, g1_ref[...])
        xo_ref[rows, :] = xn
        if with_h:
            h = _rms(xn, g2_ref[...]) * (1.0 + sc_ref[0]) + sh_ref[0]
            ho_ref[rows, :] = h.astype(ho_ref.dtype)

    def finish():
        tm = a_ref.shape[0]
        tiles = [slice(r, r + sub) for r in range(0, tm, sub)]

        def total(rows):
            part = _dot(a_ref[rows, :], w_ref[...])
            return part if nk == 1 else acc_ref[rows, :] + part

        y = total(tiles[0])
        for n, rows in enumerate(tiles):
            y_next = total(tiles[n + 1]) if n + 1 < len(tiles) else None
            epilogue(rows, y)
            y = y_next

    if nk == 1:
        finish()
        return
    k = pl.program_id(1)

    def accumulate(first):
        for r in range(0, a_ref.shape[0], sub):
            part = _dot(a_ref[r:r + sub, :], w_ref[...])
            acc_ref[r:r + sub, :] = part if first else acc_ref[r:r + sub, :] + part

    pl.when(k == 0)(functools.partial(accumulate, True))
    pl.when(jnp.logical_and(k > 0, k < nk - 1))(functools.partial(accumulate, False))
    pl.when(k == nk - 1)(finish)


def _matmul_postnorm(a, w, layer, x2, g1, gate, seq, nxt=None, tm=512, tk=None, sub=128):
    m, kdim = a.shape
    d = w.shape[-1]
    cast_w = w.dtype != BF16
    tk = kdim if tk is None else tk
    nk = kdim // tk
    assert not (cast_w and nk > 1)
    tpb = seq // tm
    with_h = nxt is not None
    row = lambda i, k: (i, 0)
    per_batch = lambda i, k: (i // tpb, 0, 0)
    const = lambda i, k: (0, 0)
    if cast_w:
        w_spec = pl.BlockSpec((None, kdim, d), lambda i, k: (layer, 0, 0), pipeline_mode=pl.Buffered(1))
    else:
        w_spec = pl.BlockSpec((None, tk, d), lambda i, k: (layer, k, 0))
    in_specs = [pl.BlockSpec((tm, tk), lambda i, k: (i, k)),
                w_spec,
                pl.BlockSpec((tm, d), row),
                pl.BlockSpec((1, d), const),
                pl.BlockSpec((1, 1, d), per_batch)]
    args = [a, w, x2, g1.reshape(1, d), gate]
    out_specs = [pl.BlockSpec((tm, d), row)]
    out_shape = [jax.ShapeDtypeStruct((m, d), F32)]
    if with_h:
        g2, sc, sh = nxt
        in_specs += [pl.BlockSpec((1, d), const), pl.BlockSpec((1, 1, d), per_batch),
                     pl.BlockSpec((1, 1, d), per_batch)]
        args += [g2.reshape(1, d), sc, sh]
        out_specs.append(pl.BlockSpec((tm, d), row))
        out_shape.append(jax.ShapeDtypeStruct((m, d), BF16))
    res = pl.pallas_call(
        functools.partial(_out_kernel, nk=nk, with_h=with_h, cast_w=cast_w, sub=sub),
        grid=(m // tm, nk),
        in_specs=in_specs,
        out_specs=out_specs,
        out_shape=out_shape,
        scratch_shapes=[pltpu.VMEM((kdim, d), BF16)] if cast_w else [],
        compiler_params=_params("arbitrary", "arbitrary"),
        name="matmul_postnorm",
    )(*args)
    return (res[0], res[1]) if with_h else (res[0], None)


def _fox_gate_kernel(h_ref, wf_ref, bf_ref, o_ref, carry_ref, *, ts):
    @pl.when(pl.program_id(1) == 0)
    def _():
        carry_ref[...] = jnp.zeros_like(carry_ref)

    logit = _dot(h_ref[...], wf_ref[...]) + bf_ref[...]
    log_f = -_softplus(-logit)
    row = lax.broadcasted_iota(jnp.int32, (ts, ts), 0)
    col = lax.broadcasted_iota(jnp.int32, (ts, ts), 1)
    tri = jnp.where(row >= col, 1.0, 0.0).astype(BF16)
    hi, mid, lo = _split3(log_f)
    cs = _dot(tri, hi) + _dot(tri, mid) + _dot(tri, lo) + carry_ref[...]
    o_ref[...] = cs
    carry_ref[...] = cs[ts - 1:ts, :]


def _fox_gate(h, wf_pad, bf_pad, batch, seq, ts=256):
    m, d = h.shape
    spb = seq // ts
    return pl.pallas_call(
        functools.partial(_fox_gate_kernel, ts=ts),
        grid=(batch, spb),
        in_specs=[pl.BlockSpec((ts, d), lambda b, s: (b * spb + s, 0)),
                  pl.BlockSpec((d, LANES), lambda b, s: (0, 0)),
                  pl.BlockSpec((1, LANES), lambda b, s: (0, 0))],
        out_specs=pl.BlockSpec((ts, LANES), lambda b, s: (b * spb + s, 0)),
        out_shape=jax.ShapeDtypeStruct((m, LANES), F32),
        scratch_shapes=[pltpu.VMEM((1, LANES), F32)],
        compiler_params=_params("parallel", "arbitrary"),
        name="fox_gate",
    )(h, wf_pad, bf_pad)


def _fox_attn_kernel(q_ref, k_ref, v_ref, cq_ref, ck_ref, o_ref, *, tq, tk, scale):
    head = pl.program_id(1)
    i = pl.program_id(2)
    ratio = tq // tk
    hd = q_ref.shape[1]
    lane = lax.broadcasted_iota(jnp.int32, (tq, LANES), 1)
    cq = jnp.sum(jnp.where(lane == head, cq_ref[...], 0.0), axis=1, keepdims=True) * LOG2E
    row_id = lax.broadcasted_iota(jnp.int32, (tk, tk), 0)
    col_id = lax.broadcasted_iota(jnp.int32, (tk, tk), 1)

    def update(j, row0, masked, carry):
        start = pl.multiple_of(j * tk, tk)
        kj = k_ref[pl.ds(start, tk), :]
        vj = v_ref[pl.ds(start, tk), :]
        ckj = ck_ref[0, 0, j] * LOG2E
        m_prev, l_prev, acc = carry
        s = _dot_nt(q_ref[row0:, :], kj) * (scale * LOG2E) + cq[row0:] - ckj
        if masked:
            top = jnp.where(row_id >= col_id, s[:tk], -jnp.inf)
            s = top if tq - row0 == tk else jnp.concatenate([top, s[tk:]], axis=0)
        m_new = jnp.maximum(m_prev, jnp.max(s, axis=1, keepdims=True))
        alpha = jnp.exp2(m_prev - m_new)
        p = jnp.exp2(s - m_new)
        l_new = alpha * l_prev + jnp.sum(p, axis=1, keepdims=True)
        return m_new, l_new, alpha * acc + _dot(p.astype(BF16), vj)

    init = (jnp.full((tq, 1), -jnp.inf, F32), jnp.zeros((tq, 1), F32), jnp.zeros((tq, hd), F32))
    carry = lax.fori_loop(0, ratio * i, lambda j, c: update(j, 0, False, c), init)
    done = []
    for t in range(ratio):
        carry = update(ratio * i + t, t * tk, True, carry)
        done.append(tuple(x[:tk] for x in carry))
        if t + 1 < ratio:
            carry = tuple(x[tk:] for x in carry)
    l_fin = jnp.concatenate([d[1] for d in done], axis=0)
    acc = jnp.concatenate([d[2] for d in done], axis=0)
    o_ref[...] = (acc / l_fin).astype(o_ref.dtype)


def _fox_attention(qkv, cum, batch, seq, heads, tq=2048, tk=256):
    m = qkv.shape[0]
    hd = qkv.shape[1] // (3 * heads)
    nq = seq // tq
    nk = seq // tk
    ck = cum[:, :heads].reshape(batch, seq, heads).transpose(0, 2, 1).reshape(batch, heads, nk, 1, tk)
    return pl.pallas_call(
        functools.partial(_fox_attn_kernel, tq=tq, tk=tk, scale=hd ** -0.5),
        grid=(batch, heads, nq),
        in_specs=[pl.BlockSpec((tq, hd), lambda b, h, i: (b * nq + i, h)),
                  pl.BlockSpec((seq, hd), lambda b, h, i: (b, heads + h)),
                  pl.BlockSpec((seq, hd), lambda b, h, i: (b, 2 * heads + h)),
                  pl.BlockSpec((tq, LANES), lambda b, h, i: (b * nq + i, 0)),
                  pl.BlockSpec((1, 1, nk, 1, tk), lambda b, h, i: (b, h, 0, 0, 0))],
        out_specs=pl.BlockSpec((tq, hd), lambda b, h, i: (b * nq + i, h)),
        out_shape=jax.ShapeDtypeStruct((m, heads * hd), BF16),
        compiler_params=_params("parallel", "parallel", "arbitrary"),
        name="fox_attention",
    )(qkv, qkv, qkv, cum, ck)


def _ffn_in_kernel(h_ref, w1_ref, w2_ref, cw1_ref, cw2_ref, cb1_ref, cb2_ref, o_ref, tail_ref, wb_ref,
                   *, tm, sub, tpb):
    @pl.when(pl.program_id(1) % tpb == 0)
    def _():
        tail_ref[...] = jnp.zeros_like(tail_ref)

    @pl.when(pl.program_id(1) == 0)
    def _():
        wb_ref[0] = w1_ref[...].astype(BF16)
        wb_ref[1] = w2_ref[...].astype(BF16)

    row = lax.broadcasted_iota(jnp.int32, (SUBLANES, 1), 0)
    w_refs = (wb_ref.at[0], wb_ref.at[1])
    cw = (cw1_ref[...], cw2_ref[...])
    cb = (cb1_ref[...], cb2_ref[...])

    def up(r):
        rows = h_ref[r * sub:(r + 1) * sub, :]
        return [_dot(rows, w_ref[...]) for w_ref in w_refs]

    def shifted(u, tail, shift):
        rolled = pltpu.roll(u, shift, 0)
        head = rolled[:SUBLANES]
        for s in range(shift):
            src = SUBLANES - shift + s
            head = jnp.where(row == s, tail[src:src + 1], head)
        return jnp.concatenate([head, rolled[SUBLANES:]], axis=0)

    def conv(n, u, tail):
        return cw[n][0:1] * shifted(u, tail, 2) + cw[n][1:2] * shifted(u, tail, 1) + cw[n][2:3] * u + cb[n]

    nsub = tm // sub
    tails = [tail_ref[0], tail_ref[1]]
    u = up(0)
    for r in range(nsub):
        u_next = up(r + 1) if r + 1 < nsub else None
        x1 = conv(0, u[0], tails[0])
        x2 = conv(1, u[1], tails[1])
        gelu = 0.5 * x1 * (1.0 + lax.erf(x1 * (2.0 ** -0.5)))
        o_ref[r * sub:(r + 1) * sub, :] = (gelu * x2).astype(o_ref.dtype)
        tails = [x[sub - SUBLANES:] for x in u]
        u = u_next
    tail_ref[0] = tails[0]
    tail_ref[1] = tails[1]


def _ffn_in(h, w_in, conv_w, conv_b, layer, seq, tm=2048, tn=512, sub=256):
    m, d = h.shape
    f = w_in.shape[2] // 2
    nf = f // tn
    tpb = seq // tm
    return pl.pallas_call(
        functools.partial(_ffn_in_kernel, tm=tm, sub=sub, tpb=tpb),
        grid=(nf, m // tm),
        in_specs=[pl.BlockSpec((tm, d), lambda j, i: (i, 0)),
                  pl.BlockSpec((None, d, tn), lambda j, i: (layer, 0, j)),
                  pl.BlockSpec((None, d, tn), lambda j, i: (layer, 0, nf + j)),
                  pl.BlockSpec((None, CONV_WIDTH, tn), lambda j, i: (layer, 0, j)),
                  pl.BlockSpec((None, CONV_WIDTH, tn), lambda j, i: (layer, 0, nf + j)),
                  pl.BlockSpec((None, 1, tn), lambda j, i: (layer, 0, j)),
                  pl.BlockSpec((None, 1, tn), lambda j, i: (layer, 0, nf + j))],
        out_specs=pl.BlockSpec((tm, tn), lambda j, i: (i, j)),
        out_shape=jax.ShapeDtypeStruct((m, f), BF16),
        scratch_shapes=[pltpu.VMEM((2, SUBLANES, tn), F32), pltpu.VMEM((2, d, tn), BF16)],
        compiler_params=_params("parallel", "arbitrary"),
        name="ffn_in",
    )(h, w_in, w_in, conv_w, conv_w, conv_b, conv_b)


def _rwkv_pre_kernel(x_ref, halo_ref, g_ref, sc_ref, sh_ref, mu_ref, *o_refs, tm, tpb):
    first = pl.program_id(0) % tpb == 0
    inv = _inv_rms(x_ref[...])
    inv_halo = _inv_rms(halo_ref[...])
    row = lax.broadcasted_iota(jnp.int32, (SUBLANES, 1), 0)
    for c in range(x_ref.shape[1] // MXU_DIM):
        cols = slice(c * MXU_DIM, (c + 1) * MXU_DIM)
        mod = lambda t, s: t * s * g_ref[:, cols] * (1.0 + sc_ref[0, :, cols]) + sh_ref[0, :, cols]
        h = mod(x_ref[:, cols], inv)
        h_halo = jnp.where(first, 0.0, mod(halo_ref[:, cols], inv_halo))
        rolled = pltpu.roll(h, 1, 0)
        head = jnp.where(row == 0, h_halo[SUBLANES - 1:SUBLANES], rolled[:SUBLANES])
        xx = jnp.concatenate([head, rolled[SUBLANES:]], axis=0) - h
        for n, o_ref in enumerate(o_refs):
            o_ref[:, cols] = (h + xx * mu_ref[n:n + 1, cols]).astype(o_ref.dtype)


def _rwkv_pre(x2, g, sc, sh, mu, seq, tm=256):
    m, d = x2.shape
    tpb = seq // tm
    hb = tm // SUBLANES
    nmix = mu.shape[0]
    return pl.pallas_call(
        functools.partial(_rwkv_pre_kernel, tm=tm, tpb=tpb),
        grid=(m // tm,),
        in_specs=[pl.BlockSpec((tm, d), lambda i: (i, 0)),
                  pl.BlockSpec((SUBLANES, d), lambda i: (jnp.maximum(i * hb - 1, 0), 0)),
                  pl.BlockSpec((1, d), lambda i: (0, 0)),
                  pl.BlockSpec((1, 1, d), lambda i: (i // tpb, 0, 0)),
                  pl.BlockSpec((1, 1, d), lambda i: (i // tpb, 0, 0)),
                  pl.BlockSpec((nmix, d), lambda i: (0, 0))],
        out_specs=[pl.BlockSpec((tm, d), lambda i: (i, 0))] * nmix,
        out_shape=[jax.ShapeDtypeStruct((m, d), BF16)] * nmix,
        compiler_params=_params("parallel"),
        name="rwkv_pre",
    )(x2, x2, g.reshape(1, d), sc, sh, mu)


def _lora_down_kernel(x_ref, w1_ref, o_ref, *, act):
    t = _dot(x_ref[...], w1_ref[...])
    if act == "tanh":
        t = jnp.tanh(t)
    elif act == "sigmoid":
        t = jax.nn.sigmoid(t)
    o_ref[...] = t.astype(o_ref.dtype)


def _lora_down(x, w1, act, tm=1024):
    m, d = x.shape
    r = w1.shape[1]
    return pl.pallas_call(
        functools.partial(_lora_down_kernel, act=act),
        grid=(m // tm,),
        in_specs=[pl.BlockSpec((tm, d), lambda i: (i, 0)),
                  pl.BlockSpec((d, r), lambda i: (0, 0))],
        out_specs=pl.BlockSpec((tm, r), lambda i: (i, 0)),
        out_shape=jax.ShapeDtypeStruct((m, r), BF16),
        compiler_params=_params("parallel"),
        name="rwkv_lora_" + act,
    )(x, w1)


def _wkv_kernel(r_ref, k_ref, v_ref, tw_ref, ta_ref, tg_ref, w2_ref, a2_ref, g2_ref,
                w0_ref, a0_ref, kk_ref, ka_ref, rk_ref, gnw_ref, gnb_ref,
                o_ref, s_ref, *, groups, chunks):
    L = WKV_CHUNK
    N = RWKV_HEAD_DIM
    W = MXU_DIM

    @pl.when(pl.program_id(2) == 0)
    def _():
        s_ref[...] = jnp.zeros_like(s_ref)

    ri = lax.broadcasted_iota(jnp.int32, (W, W), 0)
    ci = lax.broadcasted_iota(jnp.int32, (W, W), 1)
    bd = (ri // N) == (ci // N)
    bd_bf = jnp.where(bd, 1.0, 0.0).astype(BF16)
    t_i = lax.broadcasted_iota(jnp.int32, (L, W), 0)
    s_i = lax.broadcasted_iota(jnp.int32, (L, W), 1) % N
    strict = t_i > s_i
    incl = t_i >= s_i
    eye = jnp.where(t_i == s_i, 1.0, 0.0)
    tri = jnp.where(lax.broadcasted_iota(jnp.int32, (L, L), 0) >= lax.broadcasted_iota(jnp.int32, (L, L), 1),
                    1.0, 0.0).astype(BF16)

    def block_diag(x):
        xb = x.astype(BF16)
        return jnp.concatenate([xb] * (W // L), axis=0) * bd_bf

    def stack(a, b):
        return jnp.concatenate([a, b], axis=0).astype(BF16)

    def head_sums(xs):
        res = _dot(jnp.concatenate(xs, axis=0).astype(BF16), bd_bf)
        return [res[n * L:(n + 1) * L] for n in range(len(xs))]

    def each(fn, *cols):
        return [fn(*args) for args in zip(*cols)]

    def cumsum(x):
        hi, mid, lo = _split3(x)
        return _dot(tri, hi) + _dot(tri, mid) + _dot(tri, lo)

    lanes = [slice(gi * W, (gi + 1) * W) for gi in range(groups)]

    def par(ref):
        return [ref[:, sl] for sl in lanes]

    def lora_up(t_ref, w_ref):
        return [_dot(t_ref[...], w_ref[:, sl]) for sl in lanes]

    wl_all, al_all, g_all = lora_up(tw_ref, w2_ref), lora_up(ta_ref, a2_ref), lora_up(tg_ref, g2_ref)

    def prepare(c):
        rows = slice(c * L, (c + 1) * L)
        tok = lambda ref: [ref[rows, sl] for sl in lanes]
        r, v, k_raw = tok(r_ref), tok(v_ref), tok(k_ref)
        wl, al, gate = ([x[rows] for x in xs] for xs in (wl_all, al_all, g_all))
        lw = each(lambda w0, wlx: -math.exp(-0.5) * jax.nn.sigmoid(w0 + wlx), par(w0_ref), wl)
        a = each(lambda a0, alx: jax.nn.sigmoid(a0 + alx), par(a0_ref), al)
        kk = each(lambda kx, kkp: kx * kkp, k_raw, par(kk_ref))
        k = each(lambda kx, ax, ka: kx * (1.0 + (ax - 1.0) * ka), k_raw, a, par(ka_ref))
        kk = each(lambda x, sq: x / jnp.maximum(jnp.sqrt(sq), 1e-12), kk, head_sums([x * x for x in kk]))
        cum = each(cumsum, lw)
        cum_last = [cx[L - 1:L] for cx in cum]
        p_inv = each(lambda cx: jnp.exp(-cx), cum)
        decay = [jnp.exp(cl) for cl in cum_last]
        p_rest = each(lambda pi, dx: pi * dx, p_inv, decay)
        b = each(lambda x, ax: x * ax, kk, a)
        lhs_ar = each(lambda x, cx, lx, rx: stack(-x * jnp.exp(cx - lx), rx * jnp.exp(cx)), kk, cum, lw, r)
        rkr = each(lambda rx, kx, rk: rx * kx * rk, r, k, par(rk_ref))
        bonus = each(lambda sx, vx: sx * vx, head_sums(rkr), v)
        kb_rest =each(lambda bx, kx, pr: stack(bx * pr, kx * pr), b, k, p_rest)
        b_w = each(lambda bx, pi: block_diag(bx * pi), b, p_inv)
        k_w = each(lambda kx, pi: block_diag(kx * pi), k, p_inv)
        ab = each(_dot_nt, lhs_ar, b_w)
        a_ab = [jnp.where(strict, x[:L], 0.0) for x in ab]
        a_rb = [jnp.where(incl, x[L:], 0.0) for x in ab]
        inv = [eye + x for x in a_ab]
        npow = [x.astype(BF16) for x in a_ab]
        npow = each(lambda n: _dot(n, block_diag(n)).astype(BF16), npow)
        for _ in range(4):
            res = each(lambda n, t: _dot(jnp.concatenate([n, t.astype(BF16)], axis=0), block_diag(n)), npow, inv)
            npow = [x[:L].astype(BF16) for x in res]
            inv = [t + x[L:] for t, x in zip(inv, res)]
        inv = each(lambda t, n: t + _dot(t.astype(BF16), block_diag(n)), inv, npow)
        ak = each(_dot_nt, lhs_ar, k_w)
        a_ak = [jnp.where(strict, x[:L], 0.0) for x in ak]
        a_rk = [jnp.where(incl, x[L:], 0.0) for x in ak]
        a_v = each(lambda x, y, vx: _dot(stack(x, y), block_diag(vx)), a_ak, a_rk, v)
        return dict(rows=rows, v=v, lhs_ar=lhs_ar, inv=inv, a_v=a_v, a_rb=a_rb, bonus=bonus,
                    decay=decay, kb_rest=kb_rest, gate=gate)

    prepared = [prepare(c) for c in range(chunks)]
    gn_w, gn_b = par(gnw_ref), par(gnb_ref)

    state = [s_ref[gi] for gi in range(groups)]
    for q in prepared:
        a_s = each(lambda lhs, s: _dot_nt(lhs, s.astype(BF16)), q["lhs_ar"], state)
        z = each(lambda t, av, as_: _dot(t.astype(BF16), block_diag(av[:L] + as_[:L])), q["inv"], q["a_v"], a_s)
        y = each(lambda as_, av, arb, zx: as_[L:] + av[L:] + _dot(arb.astype(BF16), block_diag(zx)),
                 a_s, q["a_v"], q["a_rb"], z)
        upd = each(lambda zx, vx, kb: _dot_tn(stack(zx, vx), kb), z, q["v"], q["kb_rest"])
        state = each(lambda s, dx, ux: jnp.where(bd, s * dx + ux, 0.0), state, q["decay"], upd)

        mean = [m * (1.0 / N) for m in head_sums(y)]
        yc = each(lambda yx, mx: yx - mx, y, mean)
        var = [m * (1.0 / N) for m in head_sums([x * x for x in yc])]
        for gi, sl in enumerate(lanes):
            yn = yc[gi] * lax.rsqrt(var[gi] + GN_EPS) * gn_w[gi] + gn_b[gi]
            o_ref[q["rows"], sl] = ((yn + q["bonus"][gi]) * q["gate"][gi]).astype(o_ref.dtype)
    for gi in range(groups):
        s_ref[gi] = state[gi]


def _wkv(r, k, v, lora, w0, a0, k_k, k_a, r_k, gn_w, gn_b, batch, seq, groups=8, chunks=2):
    m, d = r.shape
    rows = chunks * WKV_CHUNK
    gw = groups * MXU_DIM
    ng = d // gw
    nc = seq // rows
    tok = pl.BlockSpec((rows, gw), lambda b, gidx, c: (b * nc + c, gidx))
    par = pl.BlockSpec((1, gw), lambda b, gidx, c: (0, gidx))
    low = [pl.BlockSpec((rows, t.shape[1]), lambda b, gidx, c: (b * nc + c, 0)) for t, _ in lora]
    up = [pl.BlockSpec((w2.shape[0], gw), lambda b, gidx, c: (0, gidx)) for _, w2 in lora]
    row = lambda t: t.reshape(1, d)
    return pl.pallas_call(
        functools.partial(_wkv_kernel, groups=groups, chunks=chunks),
        grid=(batch, ng, nc),
        in_specs=[tok] * 3 + low + up + [par] * 7,
        out_specs=tok,
        out_shape=jax.ShapeDtypeStruct((m, d), BF16),
        scratch_shapes=[pltpu.VMEM((groups, MXU_DIM, MXU_DIM), F32)],
        compiler_params=_params("parallel", "parallel", "arbitrary"),
        name="wkv7",
    )(r, k, v, *[t for t, _ in lora], *[w2 for _, w2 in lora],
      row(w0), row(a0), row(k_k), row(k_a), row(r_k), row(gn_w), row(gn_b))


def _pad_to(x, axis, size):
    pad = [(0, 0)] * x.ndim
    pad[axis] = (0, size - x.shape[axis])
    return jnp.pad(x, pad)


def _round_up(n, mult):
    return -(-n // mult) * mult


def kernel(x, c, mod_w, mod_b, norm_g, fox_w_in, fox_b_f, fox_w_out, rwkv_mu, rwkv_w0, rwkv_w1, rwkv_w2, rwkv_a0, rwkv_a1, rwkv_a2, rwkv_g1, rwkv_g2, rwkv_k_k, rwkv_k_a, rwkv_r_k, rwkv_w_r, rwkv_w_k, rwkv_w_v, rwkv_w_o, rwkv_gn_w, rwkv_gn_b, ffn_w_in, ffn_conv_w, ffn_conv_b, ffn_w_out):
    batch, seq, d = x.shape
    depth = mod_w.shape[0]
    heads = fox_b_f.shape[1]
    bf = lambda t: t.astype(BF16)

    mod = _modulation(c, mod_w, mod_b)
    x2 = x.reshape(batch * seq, d)
    ffn_w_out_bf = bf(ffn_w_out)

    h = None
    for i in range(depth):
        sh_m, sc_m, g_m, sh_f, sc_f, g_f = [mod[i, :, n] for n in range(6)]
        j = i // 2
        ffn_pre = (norm_g[i, 2], sc_f, sh_f)
        if i % 2 == 0:
            if h is None:
                h = _prenorm(x2, norm_g[i, 0], sc_m, sh_m, seq)
            qkv = _matmul(h, jnp.swapaxes(fox_w_in, 1, 2), j, 3 * d, BF16, w_is_transposed=True, name="fox_qkv")
            wf = _pad_to(bf(fox_w_in[j, :, 3 * d:]), 1, LANES)
            bfp = _pad_to(fox_b_f[j].reshape(1, heads), 1, LANES)
            cum = _fox_gate(h, wf, bfp, batch, seq)
            o = _fox_attention(qkv, cum, batch, seq, heads)
            x2, h = _matmul_postnorm(o, fox_w_out, j, x2, norm_g[i, 1], g_m, seq, nxt=ffn_pre, sub=512)
        else:
            xr, xw, xk, xv, xa, xg = _rwkv_pre(x2, norm_g[i, 0], sc_m, sh_m, rwkv_mu[j], seq)
            r = _matmul(xr, rwkv_w_r, j, d, F32, name="rwkv_r")
            k = _matmul(xk, rwkv_w_k, j, d, F32, name="rwkv_k")
            v = _matmul(xv, rwkv_w_v, j, d, F32, name="rwkv_v")
            def low_rank(xin, w1, w2, act):
                rank = _round_up(w1.shape[1], LANES)
                return _lora_down(xin, _pad_to(bf(w1), 1, rank), act), _pad_to(bf(w2), 0, rank)

            lora = [low_rank(xw, rwkv_w1[j], rwkv_w2[j], "tanh"),
                    low_rank(xa, rwkv_a1[j], rwkv_a2[j], "none"),
                    low_rank(xg, rwkv_g1[j], rwkv_g2[j], "sigmoid")]
            o = _wkv(r, k, v, lora, rwkv_w0[j], rwkv_a0[j], rwkv_k_k[j], rwkv_k_a[j],
                     rwkv_r_k[j], rwkv_gn_w[j], rwkv_gn_b[j], batch, seq)
            x2, h = _matmul_postnorm(o, rwkv_w_o, j, x2, norm_g[i, 1], g_m, seq, nxt=ffn_pre, sub=512)
        act = _ffn_in(h, ffn_w_in, ffn_conv_w, ffn_conv_b.reshape(depth, 1, -1), i, seq)
        f = act.shape[1]
        nxt = None
        if i + 1 < depth and (i + 1) % 2 == 0:
            nxt = (norm_g[i + 1, 0], mod[i + 1, :, 1], mod[i + 1, :, 0])
        x2, h = _matmul_postnorm(act, ffn_w_out_bf, i, x2, norm_g[i, 3], g_f, seq, nxt=nxt, tk=f // 2, sub=512)
    return x2.reshape(batch, seq, d)
```

```python
import functools
import math

import jax
import jax.numpy as jnp
from jax import lax
from jax.experimental import pallas as pl
from jax.experimental.pallas import tpu as pltpu

F32 = jnp.float32
BF16 = jnp.bfloat16

LOG2E = math.log2(math.e)
NORM_EPS = 1e-6
GN_EPS = 64e-5
FOX_HEADS = 16
RWKV_HEAD_DIM = 64
CONV_WIDTH = 3

LANES = 128
SUBLANES = 8
MXU_DIM = 256
WKV_CHUNK = 64
WKV_HEADS_PER_GROUP = MXU_DIM // RWKV_HEAD_DIM
VMEM_LIMIT = 58 << 20


def _params(*sem):
    return pltpu.CompilerParams(dimension_semantics=sem, vmem_limit_bytes=VMEM_LIMIT)


def _inv_rms(x):
    return lax.rsqrt(jnp.mean(x * x, axis=-1, keepdims=True) + NORM_EPS)


def _rms(x, g):
    return x * _inv_rms(x) * g


def _dot(a, b):
    return jnp.dot(a, b, preferred_element_type=F32)


def _dot_nt(a, b):
    return lax.dot_general(a, b, (((1,), (1,)), ((), ())), preferred_element_type=F32)


def _dot_tn(a, b):
    return lax.dot_general(a, b, (((0,), (0,)), ((), ())), preferred_element_type=F32)


def _split2(x):
    hi = x.astype(BF16)
    lo = (x - hi.astype(F32)).astype(BF16)
    return hi, lo


def _split3(x):
    hi = x.astype(BF16)
    r1 = x - hi.astype(F32)
    mid = r1.astype(BF16)
    lo = (r1 - mid.astype(F32)).astype(BF16)
    return hi, mid, lo


def _softplus(x):
    return jnp.maximum(x, 0.0) + jnp.log1p(jnp.exp(-jnp.abs(x)))


def _mod_kernel(c_ref, w_ref, b_ref, o_ref):
    c = c_ref[...]
    s = c * jax.nn.sigmoid(c)
    o_ref[0] = _dot(s.astype(BF16), w_ref[0].astype(BF16)) + b_ref[0]


def _modulation(c, mod_w, mod_b):
    depth, d, n = mod_w.shape
    b = c.shape[0]
    rows = -(-b // SUBLANES) * SUBLANES
    c_pad = jnp.pad(c, ((0, rows - b), (0, 0)))
    tn = 1024
    out = pl.pallas_call(
        _mod_kernel,
        grid=(depth, n // tn),
        in_specs=[pl.BlockSpec((rows, d), lambda l, j: (0, 0)),
                  pl.BlockSpec((1, d, tn), lambda l, j: (l, 0, j)),
                  pl.BlockSpec((1, 1, tn), lambda l, j: (l, 0, j))],
        out_specs=pl.BlockSpec((1, rows, tn), lambda l, j: (l, 0, j)),
        out_shape=jax.ShapeDtypeStruct((depth, rows, n), F32),
        compiler_params=_params("parallel", "parallel"),
        name="adaln_mod",
    )(c_pad, mod_w, mod_b.reshape(depth, 1, n))
    return out[:, :b].reshape(depth, b, 6, 1, d)


def _prenorm_kernel(x_ref, g_ref, sc_ref, sh_ref, o_ref):
    inv = _inv_rms(x_ref[...])
    for c in range(x_ref.shape[1] // MXU_DIM):
        cols = slice(c * MXU_DIM, (c + 1) * MXU_DIM)
        h = x_ref[:, cols] * inv * g_ref[:, cols] * (1.0 + sc_ref[0, :, cols]) + sh_ref[0, :, cols]
        o_ref[:, cols] = h.astype(o_ref.dtype)


def _prenorm(x2, g, sc, sh, seq, tm=512):
    m, d = x2.shape
    tpb = seq // tm
    return pl.pallas_call(
        _prenorm_kernel,
        grid=(m // tm,),
        in_specs=[pl.BlockSpec((tm, d), lambda i: (i, 0)),
                  pl.BlockSpec((1, d), lambda i: (0, 0)),
                  pl.BlockSpec((1, 1, d), lambda i: (i // tpb, 0, 0)),
                  pl.BlockSpec((1, 1, d), lambda i: (i // tpb, 0, 0))],
        out_specs=pl.BlockSpec((tm, d), lambda i: (i, 0)),
        out_shape=jax.ShapeDtypeStruct((m, d), BF16),
        compiler_params=_params("parallel"),
        name="prenorm",
    )(x2, g.reshape(1, d), sc, sh)


def _mm_kernel(a_ref, w_ref, o_ref, wb_ref, *, w_is_transposed):
    @pl.when(pl.program_id(1) == 0)
    def _():
        wb_ref[...] = w_ref[...].astype(BF16)

    dot = _dot_nt if w_is_transposed else _dot
    o_ref[...] = dot(a_ref[...], wb_ref[...]).astype(o_ref.dtype)


def _matmul(a, w, layer, n, out_dtype, tm=2048, tn=512, w_is_transposed=False, name="matmul"):
    m, k = a.shape
    if w_is_transposed:
        w_spec, w_tile = pl.BlockSpec((None, tn, k), lambda j, i: (layer, j, 0)), (tn, k)
    else:
        w_spec, w_tile = pl.BlockSpec((None, k, tn), lambda j, i: (layer, 0, j)), (k, tn)
    return pl.pallas_call(
        functools.partial(_mm_kernel, w_is_transposed=w_is_transposed),
        grid=(n // tn, m // tm),
        in_specs=[pl.BlockSpec((tm, k), lambda j, i: (i, 0)), w_spec],
        out_specs=pl.BlockSpec((tm, tn), lambda j, i: (i, j)),
        out_shape=jax.ShapeDtypeStruct((m, n), out_dtype),
        scratch_shapes=[pltpu.VMEM(w_tile, BF16)],
        compiler_params=_params("parallel", "arbitrary"),
        name=name,
    )(a, w)


def _out_kernel(*refs, nk, with_h, cast_w, sub):
    a_ref, w_ref, x_ref, g1_ref, gate_ref = refs[:5]
    if with_h:
        g2_ref, sc_ref, sh_ref, xo_ref, ho_ref = refs[5:10]
    else:
        xo_ref = refs[5]
    acc_ref = xo_ref
    if cast_w:
        wb_ref = refs[-1]

        @pl.when(pl.program_id(0) == 0)
        def _():
            wb_ref[...] = w_ref[...].astype(BF16)

        w_ref = wb_ref

    def epilogue(rows, y):
        xn = x_ref[rows, :] + gate_ref[0] * _rms(y, g1_ref[...])
        xo_ref[rows, :] = xn
        if with_h:
            h = _rms(xn, g2_ref[...]) * (1.0 + sc_ref[0]) + sh_ref[0]
            ho_ref[rows, :] = h.astype(ho_ref.dtype)

    def finish():
        tm = a_ref.shape[0]
        tiles = [slice(r, r + sub) for r in range(0, tm, sub)]

        def total(rows):
            part = _dot(a_ref[rows, :], w_ref[...])
            return part if nk == 1 else acc_ref[rows, :] + part

        y = total(tiles[0])
        for n, rows in enumerate(tiles):
            y_next = total(tiles[n + 1]) if n + 1 < len(tiles) else None
            epilogue(rows, y)
            y = y_next

    if nk == 1:
        finish()
        return
    k = pl.program_id(1)

    def accumulate(first):
        for r in range(0, a_ref.shape[0], sub):
            part = _dot(a_ref[r:r + sub, :], w_ref[...])
            acc_ref[r:r + sub, :] = part if first else acc_ref[r:r + sub, :] + part

    pl.when(k == 0)(functools.partial(accumulate, True))
    pl.when(jnp.logical_and(k > 0, k < nk - 1))(functools.partial(accumulate, False))
    pl.when(k == nk - 1)(finish)


def _matmul_postnorm(a, w, layer, x2, g1, gate, seq, nxt=None, tm=512, tk=None, sub=128):
    m, kdim = a.shape
    d = w.shape[-1]
    cast_w = w.dtype != BF16
    tk = kdim if tk is None else tk
    nk = kdim // tk
    assert not (cast_w and nk > 1)
    tpb = seq // tm
    with_h = nxt is not None
    row = lambda i, k: (i, 0)
    per_batch = lambda i, k: (i // tpb, 0, 0)
    const = lambda i, k: (0, 0)
    if cast_w:
        w_spec = pl.BlockSpec((None, kdim, d), lambda i, k: (layer, 0, 0), pipeline_mode=pl.Buffered(1))
    else:
        w_spec = pl.BlockSpec((None, tk, d), lambda i, k: (layer, k, 0))
    in_specs = [pl.BlockSpec((tm, tk), lambda i, k: (i, k)),
                w_spec,
                pl.BlockSpec((tm, d), row),
                pl.BlockSpec((1, d), const),
                pl.BlockSpec((1, 1, d), per_batch)]
    args = [a, w, x2, g1.reshape(1, d), gate]
    out_specs = [pl.BlockSpec((tm, d), row)]
    out_shape = [jax.ShapeDtypeStruct((m, d), F32)]
    if with_h:
        g2, sc, sh = nxt
        in_specs += [pl.BlockSpec((1, d), const), pl.BlockSpec((1, 1, d), per_batch),
                     pl.BlockSpec((1, 1, d), per_batch)]
        args += [g2.reshape(1, d), sc, sh]
        out_specs.append(pl.BlockSpec((tm, d), row))
        out_shape.append(jax.ShapeDtypeStruct((m, d), BF16))
    res = pl.pallas_call(
        functools.partial(_out_kernel, nk=nk, with_h=with_h, cast_w=cast_w, sub=sub),
        grid=(m // tm, nk),
        in_specs=in_specs,
        out_specs=out_specs,
        out_shape=out_shape,
        scratch_shapes=[pltpu.VMEM((kdim, d), BF16)] if cast_w else [],
        compiler_params=_params("arbitrary", "arbitrary"),
        name="matmul_postnorm",
    )(*args)
    return (res[0], res[1]) if with_h else (res[0], None)


def _fox_gate_kernel(h_ref, wf_ref, bf_ref, o_ref, carry_ref, *, ts):
    @pl.when(pl.program_id(1) == 0)
    def _():
        carry_ref[...] = jnp.zeros_like(carry_ref)

    logit = _dot(h_ref[...], wf_ref[...]) + bf_ref[...]
    log_f = -_softplus(-logit)
    row = lax.broadcasted_iota(jnp.int32, (ts, ts), 0)
    col = lax.broadcasted_iota(jnp.int32, (ts, ts), 1)
    tri = jnp.where(row >= col, 1.0, 0.0).astype(BF16)
    hi, mid, lo = _split3(log_f)
    cs = _dot(tri, hi) + _dot(tri, mid) + _dot(tri, lo) + carry_ref[...]
    o_ref[...] = cs
    carry_ref[...] = cs[ts - 1:ts, :]


def _fox_gate(h, wf_pad, bf_pad, batch, seq, ts=256):
    m, d = h.shape
    spb = seq // ts
    return pl.pallas_call(
        functools.partial(_fox_gate_kernel, ts=ts),
        grid=(batch, spb),
        in_specs=[pl.BlockSpec((ts, d), lambda b, s: (b * spb + s, 0)),
                  pl.BlockSpec((d, LANES), lambda b, s: (0, 0)),
                  pl.BlockSpec((1, LANES), lambda b, s: (0, 0))],
        out_specs=pl.BlockSpec((ts, LANES), lambda b, s: (b * spb + s, 0)),
        out_shape=jax.ShapeDtypeStruct((m, LANES), F32),
        scratch_shapes=[pltpu.VMEM((1, LANES), F32)],
        compiler_params=_params("parallel", "arbitrary"),
        name="fox_gate",
    )(h, wf_pad, bf_pad)


def _fox_attn_kernel(q_ref, k_ref, v_ref, cq_ref, ck_ref, o_ref, *, tq, tk, scale):
    head = pl.program_id(1)
    i = pl.program_id(2)
    ratio = tq // tk
    hd = q_ref.shape[1]
    lane = lax.broadcasted_iota(jnp.int32, (tq, LANES), 1)
    cq = jnp.sum(jnp.where(lane == head, cq_ref[...], 0.0), axis=1, keepdims=True) * LOG2E
    row_id = lax.broadcasted_iota(jnp.int32, (tk, tk), 0)
    col_id = lax.broadcasted_iota(jnp.int32, (tk, tk), 1)

    def update(j, row0, masked, carry):
        start = pl.multiple_of(j * tk, tk)
        kj = k_ref[pl.ds(start, tk), :]
        vj = v_ref[pl.ds(start, tk), :]
        ckj = ck_ref[0, 0, j] * LOG2E
        m_prev, l_prev, acc = carry
        s = _dot_nt(q_ref[row0:, :], kj) * (scale * LOG2E) + cq[row0:] - ckj
        if masked:
            top = jnp.where(row_id >= col_id, s[:tk], -jnp.inf)
            s = top if tq - row0 == tk else jnp.concatenate([top, s[tk:]], axis=0)
        m_new = jnp.maximum(m_prev, jnp.max(s, axis=1, keepdims=True))
        alpha = jnp.exp2(m_prev - m_new)
        p = jnp.exp2(s - m_new)
        l_new = alpha * l_prev + jnp.sum(p, axis=1, keepdims=True)
        return m_new, l_new, alpha * acc + _dot(p.astype(BF16), vj)

    init = (jnp.full((tq, 1), -jnp.inf, F32), jnp.zeros((tq, 1), F32), jnp.zeros((tq, hd), F32))
    carry = lax.fori_loop(0, ratio * i, lambda j, c: update(j, 0, False, c), init)
    done = []
    for t in range(ratio):
        carry = update(ratio * i + t, t * tk, True, carry)
        done.append(tuple(x[:tk] for x in carry))
        if t + 1 < ratio:
            carry = tuple(x[tk:] for x in carry)
    l_fin = jnp.concatenate([d[1] for d in done], axis=0)
    acc = jnp.concatenate([d[2] for d in done], axis=0)
    o_ref[...] = (acc / l_fin).astype(o_ref.dtype)


def _fox_attention(qkv, cum, batch, seq, heads, tq=2048, tk=256):
    m = qkv.shape[0]
    hd = qkv.shape[1] // (3 * heads)
    nq = seq // tq
    nk = seq // tk
    ck = cum[:, :heads].reshape(batch, seq, heads).transpose(0, 2, 1).reshape(batch, heads, nk, 1, tk)
    return pl.pallas_call(
        functools.partial(_fox_attn_kernel, tq=tq, tk=tk, scale=hd ** -0.5),
        grid=(batch, heads, nq),
        in_specs=[pl.BlockSpec((tq, hd), lambda b, h, i: (b * nq + i, h)),
                  pl.BlockSpec((seq, hd), lambda b, h, i: (b, heads + h)),
                  pl.BlockSpec((seq, hd), lambda b, h, i: (b, 2 * heads + h)),
                  pl.BlockSpec((tq, LANES), lambda b, h, i: (b * nq + i, 0)),
                  pl.BlockSpec((1, 1, nk, 1, tk), lambda b, h, i: (b, h, 0, 0, 0))],
        out_specs=pl.BlockSpec((tq, hd), lambda b, h, i: (b * nq + i, h)),
        out_shape=jax.ShapeDtypeStruct((m, heads * hd), BF16),
        compiler_params=_params("parallel", "parallel", "arbitrary"),
        name="fox_attention",
    )(qkv, qkv, qkv, cum, ck)


def _ffn_in_kernel(h_ref, w1_ref, w2_ref, cw1_ref, cw2_ref, cb1_ref, cb2_ref, o_ref, tail_ref, wb_ref,
                   *, tm, sub, tpb):
    tn = o_ref.shape[1]

    @pl.when(pl.program_id(1) % tpb == 0)
    def _():
        tail_ref[...] = jnp.zeros_like(tail_ref)

    @pl.when(pl.program_id(1) == 0)
    def _():
        wb_ref[:, :tn] = w1_ref[...].astype(BF16)
        wb_ref[:, tn:] = w2_ref[...].astype(BF16)

    row = lax.broadcasted_iota(jnp.int32, (SUBLANES, 1), 0)
    cw = (cw1_ref[...], cw2_ref[...])
    cb = (cb1_ref[...], cb2_ref[...])

    def up(r):
        u = _dot(h_ref[r * sub:(r + 1) * sub, :], wb_ref[...])
        return [u[:, :tn], u[:, tn:]]

    def shifted(u, tail, shift):
        rolled = pltpu.roll(u, shift, 0)
        head = rolled[:SUBLANES]
        for s in range(shift):
            src = SUBLANES - shift + s
            head = jnp.where(row == s, tail[src:src + 1], head)
        return jnp.concatenate([head, rolled[SUBLANES:]], axis=0)

    def conv(n, u, tail):
        return cw[n][0:1] * shifted(u, tail, 2) + cw[n][1:2] * shifted(u, tail, 1) + cw[n][2:3] * u + cb[n]

    nsub = tm // sub
    tails = [tail_ref[0], tail_ref[1]]
    u = up(0)
    for r in range(nsub):
        u_next = up(r + 1) if r + 1 < nsub else None
        x1 = conv(0, u[0], tails[0])
        x2 = conv(1, u[1], tails[1])
        gelu = 0.5 * x1 * (1.0 + lax.erf(x1 * (2.0 ** -0.5)))
        o_ref[r * sub:(r + 1) * sub, :] = (gelu * x2).astype(o_ref.dtype)
        tails = [x[sub - SUBLANES:] for x in u]
        u = u_next
    tail_ref[0] = tails[0]
    tail_ref[1] = tails[1]


def _ffn_in(h, w_in, conv_w, conv_b, layer, seq, tm=2048, tn=512, sub=256):
    m, d = h.shape
    f = w_in.shape[2] // 2
    nf = f // tn
    tpb = seq // tm
    return pl.pallas_call(
        functools.partial(_ffn_in_kernel, tm=tm, sub=sub, tpb=tpb),
        grid=(nf, m // tm),
        in_specs=[pl.BlockSpec((tm, d), lambda j, i: (i, 0)),
                  pl.BlockSpec((None, d, tn), lambda j, i: (layer, 0, j)),
                  pl.BlockSpec((None, d, tn), lambda j, i: (layer, 0, nf + j)),
                  pl.BlockSpec((None, CONV_WIDTH, tn), lambda j, i: (layer, 0, j)),
                  pl.BlockSpec((None, CONV_WIDTH, tn), lambda j, i: (layer, 0, nf + j)),
                  pl.BlockSpec((None, 1, tn), lambda j, i: (layer, 0, j)),
                  pl.BlockSpec((None, 1, tn), lambda j, i: (layer, 0, nf + j))],
        out_specs=pl.BlockSpec((tm, tn), lambda j, i: (i, j)),
        out_shape=jax.ShapeDtypeStruct((m, f), BF16),
        scratch_shapes=[pltpu.VMEM((2, SUBLANES, tn), F32), pltpu.VMEM((d, 2 * tn), BF16)],
        compiler_params=_params("parallel", "arbitrary"),
        name="ffn_in",
    )(h, w_in, w_in, conv_w, conv_w, conv_b, conv_b)


def _rwkv_pre_kernel(x_ref, halo_ref, g_ref, sc_ref, sh_ref, mu_ref, *o_refs, tm, tpb):
    first = pl.program_id(0) % tpb == 0
    inv = _inv_rms(x_ref[...])
    inv_halo = _inv_rms(halo_ref[...])
    row = lax.broadcasted_iota(jnp.int32, (SUBLANES, 1), 0)
    for c in range(x_ref.shape[1] // MXU_DIM):
        cols = slice(c * MXU_DIM, (c + 1) * MXU_DIM)
        mod = lambda t, s: t * s * g_ref[:, cols] * (1.0 + sc_ref[0, :, cols]) + sh_ref[0, :, cols]
        h = mod(x_ref[:, cols], inv)
        h_halo = jnp.where(first, 0.0, mod(halo_ref[:, cols], inv_halo))
        rolled = pltpu.roll(h, 1, 0)
        head = jnp.where(row == 0, h_halo[SUBLANES - 1:SUBLANES], rolled[:SUBLANES])
        xx = jnp.concatenate([head, rolled[SUBLANES:]], axis=0) - h
        for n, o_ref in enumerate(o_refs):
            o_ref[:, cols] = (h + xx * mu_ref[n:n + 1, cols]).astype(o_ref.dtype)


def _rwkv_pre(x2, g, sc, sh, mu, seq, tm=256):
    m, d = x2.shape
    tpb = seq // tm
    hb = tm // SUBLANES
    nmix = mu.shape[0]
    return pl.pallas_call(
        functools.partial(_rwkv_pre_kernel, tm=tm, tpb=tpb),
        grid=(m // tm,),
        in_specs=[pl.BlockSpec((tm, d), lambda i: (i, 0)),
                  pl.BlockSpec((SUBLANES, d), lambda i: (jnp.maximum(i * hb - 1, 0), 0)),
                  pl.BlockSpec((1, d), lambda i: (0, 0)),
                  pl.BlockSpec((1, 1, d), lambda i: (i // tpb, 0, 0)),
                  pl.BlockSpec((1, 1, d), lambda i: (i // tpb, 0, 0)),
                  pl.BlockSpec((nmix, d), lambda i: (0, 0))],
        out_specs=[pl.BlockSpec((tm, d), lambda i: (i, 0))] * nmix,
        out_shape=[jax.ShapeDtypeStruct((m, d), BF16)] * nmix,
        compiler_params=_params("parallel"),
        name="rwkv_pre",
    )(x2, x2, g.reshape(1, d), sc, sh, mu)


def _lora_down_kernel(x_ref, w1_ref, o_ref, *, act):
    t = _dot(x_ref[...], w1_ref[...])
    if act == "tanh":
        t = jnp.tanh(t)
    elif act == "sigmoid":
        t = jax.nn.sigmoid(t)
    o_ref[...] = t.astype(o_ref.dtype)


def _lora_down(x, w1, act, tm=1024):
    m, d = x.shape
    r = w1.shape[1]
    return pl.pallas_call(
        functools.partial(_lora_down_kernel, act=act),
        grid=(m // tm,),
        in_specs=[pl.BlockSpec((tm, d), lambda i: (i, 0)),
                  pl.BlockSpec((d, r), lambda i: (0, 0))],
        out_specs=pl.BlockSpec((tm, r), lambda i: (i, 0)),
        out_shape=jax.ShapeDtypeStruct((m, r), BF16),
        compiler_params=_params("parallel"),
        name="rwkv_lora_" + act,
    )(x, w1)


def _wkv_kernel(r_ref, k_ref, v_ref, tw_ref, ta_ref, tg_ref, w2_ref, a2_ref, g2_ref,
                w0_ref, a0_ref, kk_ref, ka_ref, rk_ref, gnw_ref, gnb_ref,
                o_ref, s_ref, *, groups, chunks):
    L = WKV_CHUNK
    N = RWKV_HEAD_DIM
    W = MXU_DIM

    @pl.when(pl.program_id(2) == 0)
    def _():
        s_ref[...] = jnp.zeros_like(s_ref)

    ri = lax.broadcasted_iota(jnp.int32, (W, W), 0)
    ci = lax.broadcasted_iota(jnp.int32, (W, W), 1)
    bd = (ri // N) == (ci // N)
    bd_bf = jnp.where(bd, 1.0, 0.0).astype(BF16)
    t_i = lax.broadcasted_iota(jnp.int32, (L, W), 0)
    s_i = lax.broadcasted_iota(jnp.int32, (L, W), 1) % N
    strict = t_i > s_i
    incl = t_i >= s_i
    eye = jnp.where(t_i == s_i, 1.0, 0.0)
    tri = jnp.where(lax.broadcasted_iota(jnp.int32, (L, L), 0) >= lax.broadcasted_iota(jnp.int32, (L, L), 1),
                    1.0, 0.0).astype(BF16)

    def block_diag(x):
        xb = x.astype(BF16)
        return jnp.concatenate([xb] * (W // L), axis=0) * bd_bf

    def stack(a, b):
        return jnp.concatenate([a, b], axis=0).astype(BF16)

    def head_sums(xs):
        res = _dot(jnp.concatenate(xs, axis=0).astype(BF16), bd_bf)
        return [res[n * L:(n + 1) * L] for n in range(len(xs))]

    def each(fn, *cols):
        return [fn(*args) for args in zip(*cols)]

    def cumsum(x):
        hi, mid, lo = _split3(x)
        return _dot(tri, hi) + _dot(tri, mid) + _dot(tri, lo)

    lanes = [slice(gi * W, (gi + 1) * W) for gi in range(groups)]

    def par(ref):
        return [ref[:, sl] for sl in lanes]

    def lora_up(t_ref, w_ref):
        return [_dot(t_ref[...], w_ref[:, sl]) for sl in lanes]

    wl_all, al_all, g_all = lora_up(tw_ref, w2_ref), lora_up(ta_ref, a2_ref), lora_up(tg_ref, g2_ref)

    def prepare(c):
        rows = slice(c * L, (c + 1) * L)
        tok = lambda ref: [ref[rows, sl] for sl in lanes]
        r, v, k_raw = tok(r_ref), tok(v_ref), tok(k_ref)
        wl, al, gate = ([x[rows] for x in xs] for xs in (wl_all, al_all, g_all))
        lw = each(lambda w0, wlx: -math.exp(-0.5) * jax.nn.sigmoid(w0 + wlx), par(w0_ref), wl)
        a = each(lambda a0, alx: jax.nn.sigmoid(a0 + alx), par(a0_ref), al)
        kk = each(lambda kx, kkp: kx * kkp, k_raw, par(kk_ref))
        k = each(lambda kx, ax, ka: kx * (1.0 + (ax - 1.0) * ka), k_raw, a, par(ka_ref))
        kk = each(lambda x, sq: x / jnp.maximum(jnp.sqrt(sq), 1e-12), kk, head_sums([x * x for x in kk]))
        cum = each(cumsum, lw)
        cum_last = [cx[L - 1:L] for cx in cum]
        p_inv = each(lambda cx: jnp.exp(-cx), cum)
        decay = [jnp.exp(cl) for cl in cum_last]
        p_rest = each(lambda pi, dx: pi * dx, p_inv, decay)
        b = each(lambda x, ax: x * ax, kk, a)
        lhs_ar = each(lambda x, cx, lx, rx: stack(-x * jnp.exp(cx - lx), rx * jnp.exp(cx)), kk, cum, lw, r)
        rkr = each(lambda rx, kx, rk: rx * kx * rk, r, k, par(rk_ref))
        bonus = each(lambda sx, vx: sx * vx, head_sums(rkr), v)
        kb_rest =each(lambda bx, kx, pr: stack(bx * pr, kx * pr), b, k, p_rest)
        b_w = each(lambda bx, pi: block_diag(bx * pi), b, p_inv)
        k_w = each(lambda kx, pi: block_diag(kx * pi), k, p_inv)
        ab = each(_dot_nt, lhs_ar, b_w)
        a_ab = [jnp.where(strict, x[:L], 0.0) for x in ab]
        a_rb = [jnp.where(incl, x[L:], 0.0) for x in ab]
        inv = [eye + x for x in a_ab]
        npow = [x.astype(BF16) for x in a_ab]
        npow = each(lambda n: _dot(n, block_diag(n)).astype(BF16), npow)
        for _ in range(4):
            res = each(lambda n, t: _dot(jnp.concatenate([n, t.astype(BF16)], axis=0), block_diag(n)), npow, inv)
            npow = [x[:L].astype(BF16) for x in res]
            inv = [t + x[L:] for t, x in zip(inv, res)]
        inv = each(lambda t, n: t + _dot(t.astype(BF16), block_diag(n)), inv, npow)
        ak = each(_dot_nt, lhs_ar, k_w)
        a_ak = [jnp.where(strict, x[:L], 0.0) for x in ak]
        a_rk = [jnp.where(incl, x[L:], 0.0) for x in ak]
        a_v = each(lambda x, y, vx: _dot(stack(x, y), block_diag(vx)), a_ak, a_rk, v)
        return dict(rows=rows, v=v, lhs_ar=lhs_ar, inv=inv, a_v=a_v, a_rb=a_rb, bonus=bonus,
                    decay=decay, kb_rest=kb_rest, gate=gate)

    prepared = [prepare(c) for c in range(chunks)]
    gn_w, gn_b = par(gnw_ref), par(gnb_ref)

    state = [s_ref[gi] for gi in range(groups)]
    for q in prepared:
        a_s = each(lambda lhs, s: _dot_nt(lhs, s.astype(BF16)), q["lhs_ar"], state)
        z = each(lambda t, av, as_: _dot(t.astype(BF16), block_diag(av[:L] + as_[:L])), q["inv"], q["a_v"], a_s)
        y = each(lambda as_, av, arb, zx: as_[L:] + av[L:] + _dot(arb.astype(BF16), block_diag(zx)),
                 a_s, q["a_v"], q["a_rb"], z)
        upd = each(lambda zx, vx, kb: _dot_tn(stack(zx, vx), kb), z, q["v"], q["kb_rest"])
        state = each(lambda s, dx, ux: jnp.where(bd, s * dx + ux, 0.0), state, q["decay"], upd)

        mean = [m * (1.0 / N) for m in head_sums(y)]
        yc = each(lambda yx, mx: yx - mx, y, mean)
        var = [m * (1.0 / N) for m in head_sums([x * x for x in yc])]
        for gi, sl in enumerate(lanes):
            yn = yc[gi] * lax.rsqrt(var[gi] + GN_EPS) * gn_w[gi] + gn_b[gi]
            o_ref[q["rows"], sl] = ((yn + q["bonus"][gi]) * q["gate"][gi]).astype(o_ref.dtype)
    for gi in range(groups):
        s_ref[gi] = state[gi]


def _wkv(r, k, v, lora, w0, a0, k_k, k_a, r_k, gn_w, gn_b, batch, seq, groups=8, chunks=2):
    m, d = r.shape
    rows = chunks * WKV_CHUNK
    gw = groups * MXU_DIM
    ng = d // gw
    nc = seq // rows
    tok = pl.BlockSpec((rows, gw), lambda b, gidx, c: (b * nc + c, gidx))
    par = pl.BlockSpec((1, gw), lambda b, gidx, c: (0, gidx))
    low = [pl.BlockSpec((rows, t.shape[1]), lambda b, gidx, c: (b * nc + c, 0)) for t, _ in lora]
    up = [pl.BlockSpec((w2.shape[0], gw), lambda b, gidx, c: (0, gidx)) for _, w2 in lora]
    row = lambda t: t.reshape(1, d)
    return pl.pallas_call(
        functools.partial(_wkv_kernel, groups=groups, chunks=chunks),
        grid=(batch, ng, nc),
        in_specs=[tok] * 3 + low + up + [par] * 7,
        out_specs=tok,
        out_shape=jax.ShapeDtypeStruct((m, d), BF16),
        scratch_shapes=[pltpu.VMEM((groups, MXU_DIM, MXU_DIM), F32)],
        compiler_params=_params("parallel", "parallel", "arbitrary"),
        name="wkv7",
    )(r, k, v, *[t for t, _ in lora], *[w2 for _, w2 in lora],
      row(w0), row(a0), row(k_k), row(k_a), row(r_k), row(gn_w), row(gn_b))


def _pad_to(x, axis, size):
    pad = [(0, 0)] * x.ndim
    pad[axis] = (0, size - x.shape[axis])
    return jnp.pad(x, pad)


def _round_up(n, mult):
    return -(-n // mult) * mult


def kernel(x, c, mod_w, mod_b, norm_g, fox_w_in, fox_b_f, fox_w_out, rwkv_mu, rwkv_w0, rwkv_w1, rwkv_w2, rwkv_a0, rwkv_a1, rwkv_a2, rwkv_g1, rwkv_g2, rwkv_k_k, rwkv_k_a, rwkv_r_k, rwkv_w_r, rwkv_w_k, rwkv_w_v, rwkv_w_o, rwkv_gn_w, rwkv_gn_b, ffn_w_in, ffn_conv_w, ffn_conv_b, ffn_w_out):
    batch, seq, d = x.shape
    depth = mod_w.shape[0]
    heads = fox_b_f.shape[1]
    bf = lambda t: t.astype(BF16)

    mod = _modulation(c, mod_w, mod_b)
    x2 = x.reshape(batch * seq, d)
    ffn_w_out_bf = bf(ffn_w_out)

    h = None
    for i in range(depth):
        sh_m, sc_m, g_m, sh_f, sc_f, g_f = [mod[i, :, n] for n in range(6)]
        j = i // 2
        ffn_pre = (norm_g[i, 2], sc_f, sh_f)
        if i % 2 == 0:
            if h is None:
                h = _prenorm(x2, norm_g[i, 0], sc_m, sh_m, seq)
            qkv = _matmul(h, jnp.swapaxes(fox_w_in, 1, 2), j, 3 * d, BF16, w_is_transposed=True, name="fox_qkv")
            wf = _pad_to(bf(fox_w_in[j, :, 3 * d:]), 1, LANES)
            bfp = _pad_to(fox_b_f[j].reshape(1, heads), 1, LANES)
            cum = _fox_gate(h, wf, bfp, batch, seq)
            o = _fox_attention(qkv, cum, batch, seq, heads)
            x2, h = _matmul_postnorm(o, fox_w_out, j, x2, norm_g[i, 1], g_m, seq, nxt=ffn_pre, sub=512)
        else:
            xr, xw, xk, xv, xa, xg = _rwkv_pre(x2, norm_g[i, 0], sc_m, sh_m, rwkv_mu[j], seq)
            r = _matmul(xr, rwkv_w_r, j, d, F32, name="rwkv_r")
            k = _matmul(xk, rwkv_w_k, j, d, F32, name="rwkv_k")
            v = _matmul(xv, rwkv_w_v, j, d, F32, name="rwkv_v")
            def low_rank(xin, w1, w2, act):
                rank = _round_up(w1.shape[1], LANES)
                return _lora_down(xin, _pad_to(bf(w1), 1, rank), act), _pad_to(bf(w2), 0, rank)

            lora = [low_rank(xw, rwkv_w1[j], rwkv_w2[j], "tanh"),
                    low_rank(xa, rwkv_a1[j], rwkv_a2[j], "none"),
                    low_rank(xg, rwkv_g1[j], rwkv_g2[j], "sigmoid")]
            o = _wkv(r, k, v, lora, rwkv_w0[j], rwkv_a0[j], rwkv_k_k[j], rwkv_k_a[j],
                     rwkv_r_k[j], rwkv_gn_w[j], rwkv_gn_b[j], batch, seq)
            x2, h = _matmul_postnorm(o, rwkv_w_o, j, x2, norm_g[i, 1], g_m, seq, nxt=ffn_pre, sub=512)
        act = _ffn_in(h, ffn_w_in, ffn_conv_w, ffn_conv_b.reshape(depth, 1, -1), i, seq)
        f = act.shape[1]
        nxt = None
        if i + 1 < depth and (i + 1) % 2 == 0:
            nxt = (norm_g[i + 1, 0], mod[i + 1, :, 1], mod[i + 1, :, 0])
        x2, h = _matmul_postnorm(act, ffn_w_out_bf, i, x2, norm_g[i, 3], g_f, seq, nxt=nxt, tk=f // 2, sub=512)
    return x2.reshape(batch, seq, d)
```

```python
import functools
import math

import jax
import jax.numpy as jnp
from jax import lax
from jax.experimental import pallas as pl
from jax.experimental.pallas import tpu as pltpu

F32 = jnp.float32
BF16 = jnp.bfloat16

LOG2E = math.log2(math.e)
NORM_EPS = 1e-6
GN_EPS = 64e-5
FOX_HEADS = 16
RWKV_HEAD_DIM = 64
CONV_WIDTH = 3

LANES = 128
SUBLANES = 8
MXU_DIM = 256
WKV_CHUNK = 64
WKV_HEADS_PER_GROUP = MXU_DIM // RWKV_HEAD_DIM
VMEM_LIMIT = 58 << 20


def _params(*sem):
    return pltpu.CompilerParams(dimension_semantics=sem, vmem_limit_bytes=VMEM_LIMIT)


def _inv_rms(x):
    return lax.rsqrt(jnp.mean(x * x, axis=-1, keepdims=True) + NORM_EPS)


def _rms(x, g):
    return x * _inv_rms(x) * g


def _dot(a, b):
    return jnp.dot(a, b, preferred_element_type=F32)


def _dot_nt(a, b):
    return lax.dot_general(a, b, (((1,), (1,)), ((), ())), preferred_element_type=F32)


def _dot_tn(a, b):
    return lax.dot_general(a, b, (((0,), (0,)), ((), ())), preferred_element_type=F32)


def _split2(x):
    hi = x.astype(BF16)
    lo = (x - hi.astype(F32)).astype(BF16)
    return hi, lo


def _split3(x):
    hi = x.astype(BF16)
    r1 = x - hi.astype(F32)
    mid = r1.astype(BF16)
    lo = (r1 - mid.astype(F32)).astype(BF16)
    return hi, mid, lo


def _softplus(x):
    return jnp.maximum(x, 0.0) + jnp.log1p(jnp.exp(-jnp.abs(x)))


def _mod_kernel(c_ref, w_ref, b_ref, o_ref):
    c = c_ref[...]
    s = c * jax.nn.sigmoid(c)
    o_ref[0] = _dot(s.astype(BF16), w_ref[0].astype(BF16)) + b_ref[0]


def _modulation(c, mod_w, mod_b):
    depth, d, n = mod_w.shape
    b = c.shape[0]
    rows = -(-b // SUBLANES) * SUBLANES
    c_pad = jnp.pad(c, ((0, rows - b), (0, 0)))
    tn = 1024
    out = pl.pallas_call(
        _mod_kernel,
        grid=(depth, n // tn),
        in_specs=[pl.BlockSpec((rows, d), lambda l, j: (0, 0)),
                  pl.BlockSpec((1, d, tn), lambda l, j: (l, 0, j)),
                  pl.BlockSpec((1, 1, tn), lambda l, j: (l, 0, j))],
        out_specs=pl.BlockSpec((1, rows, tn), lambda l, j: (l, 0, j)),
        out_shape=jax.ShapeDtypeStruct((depth, rows, n), F32),
        compiler_params=_params("parallel", "parallel"),
        name="adaln_mod",
    )(c_pad, mod_w, mod_b.reshape(depth, 1, n))
    return out[:, :b].reshape(depth, b, 6, 1, d)


def _prenorm_kernel(x_ref, g_ref, sc_ref, sh_ref, o_ref):
    inv = _inv_rms(x_ref[...])
    for c in range(x_ref.shape[1] // MXU_DIM):
        cols = slice(c * MXU_DIM, (c + 1) * MXU_DIM)
        h = x_ref[:, cols] * inv * g_ref[:, cols] * (1.0 + sc_ref[0, :, cols]) + sh_ref[0, :, cols]
        o_ref[:, cols] = h.astype(o_ref.dtype)


def _prenorm(x2, g, sc, sh, seq, tm=512):
    m, d = x2.shape
    tpb = seq // tm
    return pl.pallas_call(
        _prenorm_kernel,
        grid=(m // tm,),
        in_specs=[pl.BlockSpec((tm, d), lambda i: (i, 0)),
                  pl.BlockSpec((1, d), lambda i: (0, 0)),
                  pl.BlockSpec((1, 1, d), lambda i: (i // tpb, 0, 0)),
                  pl.BlockSpec((1, 1, d), lambda i: (i // tpb, 0, 0))],
        out_specs=pl.BlockSpec((tm, d), lambda i: (i, 0)),
        out_shape=jax.ShapeDtypeStruct((m, d), BF16),
        compiler_params=_params("parallel"),
        name="prenorm",
    )(x2, g.reshape(1, d), sc, sh)


def _mm_kernel(a_ref, w_ref, o_ref, wb_ref, *, w_is_transposed):
    @pl.when(pl.program_id(1) == 0)
    def _():
        wb_ref[...] = w_ref[...].astype(BF16)

    dot = _dot_nt if w_is_transposed else _dot
    o_ref[...] = dot(a_ref[...], wb_ref[...]).astype(o_ref.dtype)


def _matmul(a, w, layer, n, out_dtype, tm=2048, tn=512, w_is_transposed=False, name="matmul"):
    m, k = a.shape
    if w_is_transposed:
        w_spec, w_tile = pl.BlockSpec((None, tn, k), lambda j, i: (layer, j, 0)), (tn, k)
    else:
        w_spec, w_tile = pl.BlockSpec((None, k, tn), lambda j, i: (layer, 0, j)), (k, tn)
    return pl.pallas_call(
        functools.partial(_mm_kernel, w_is_transposed=w_is_transposed),
        grid=(n // tn, m // tm),
        in_specs=[pl.BlockSpec((tm, k), lambda j, i: (i, 0)), w_spec],
        out_specs=pl.BlockSpec((tm, tn), lambda j, i: (i, j)),
        out_shape=jax.ShapeDtypeStruct((m, n), out_dtype),
        scratch_shapes=[pltpu.VMEM(w_tile, BF16)],
        compiler_params=_params("parallel", "arbitrary"),
        name=name,
    )(a, w)


def _out_kernel(*refs, nk, with_h, cast_w, sub):
    a_ref, w_ref, x_ref, g1_ref, gate_ref = refs[:5]
    if with_h:
        g2_ref, sc_ref, sh_ref, xo_ref, ho_ref = refs[5:10]
    else:
        xo_ref = refs[5]
    acc_ref = xo_ref
    if cast_w:
        wb_ref = refs[-1]

        @pl.when(pl.program_id(0) == 0)
        def _():
            wb_ref[...] = w_ref[...].astype(BF16)

        w_ref = wb_ref

    def epilogue(rows, y):
        xn = x_ref[rows, :] + gate_ref[0] * _rms(y, g1_ref[...])
        xo_ref[rows, :] = xn
        if with_h:
            h = _rms(xn, g2_ref[...]) * (1.0 + sc_ref[0]) + sh_ref[0]
            ho_ref[rows, :] = h.astype(ho_ref.dtype)

    def finish():
        tm = a_ref.shape[0]
        tiles = [slice(r, r + sub) for r in range(0, tm, sub)]

        def total(rows):
            part = _dot(a_ref[rows, :], w_ref[...])
            return part if nk == 1 else acc_ref[rows, :] + part

        y = total(tiles[0])
        for n, rows in enumerate(tiles):
            y_next = total(tiles[n + 1]) if n + 1 < len(tiles) else None
            epilogue(rows, y)
            y = y_next

    if nk == 1:
        finish()
        return
    k = pl.program_id(1)

    def accumulate(first):
        for r in range(0, a_ref.shape[0], sub):
            part = _dot(a_ref[r:r + sub, :], w_ref[...])
            acc_ref[r:r + sub, :] = part if first else acc_ref[r:r + sub, :] + part

    pl.when(k == 0)(functools.partial(accumulate, True))
    pl.when(jnp.logical_and(k > 0, k < nk - 1))(functools.partial(accumulate, False))
    pl.when(k == nk - 1)(finish)


def _matmul_postnorm(a, w, layer, x2, g1, gate, seq, nxt=None, tm=512, tk=None, sub=128):
    m, kdim = a.shape
    d = w.shape[-1]
    cast_w = w.dtype != BF16
    tk = kdim if tk is None else tk
    nk = kdim // tk
    assert not (cast_w and nk > 1)
    tpb = seq // tm
    with_h = nxt is not None
    row = lambda i, k: (i, 0)
    per_batch = lambda i, k: (i // tpb, 0, 0)
    const = lambda i, k: (0, 0)
    if cast_w:
        w_spec = pl.BlockSpec((None, kdim, d), lambda i, k: (layer, 0, 0), pipeline_mode=pl.Buffered(1))
    else:
        w_spec = pl.BlockSpec((None, tk, d), lambda i, k: (layer, k, 0))
    in_specs = [pl.BlockSpec((tm, tk), lambda i, k: (i, k)),
                w_spec,
                pl.BlockSpec((tm, d), row),
                pl.BlockSpec((1, d), const),
                pl.BlockSpec((1, 1, d), per_batch)]
    args = [a, w, x2, g1.reshape(1, d), gate]
    out_specs = [pl.BlockSpec((tm, d), row)]
    out_shape = [jax.ShapeDtypeStruct((m, d), F32)]
    if with_h:
        g2, sc, sh = nxt
        in_specs += [pl.BlockSpec((1, d), const), pl.BlockSpec((1, 1, d), per_batch),
                     pl.BlockSpec((1, 1, d), per_batch)]
        args += [g2.reshape(1, d), sc, sh]
        out_specs.append(pl.BlockSpec((tm, d), row))
        out_shape.append(jax.ShapeDtypeStruct((m, d), BF16))
    res = pl.pallas_call(
        functools.partial(_out_kernel, nk=nk, with_h=with_h, cast_w=cast_w, sub=sub),
        grid=(m // tm, nk),
        in_specs=in_specs,
        out_specs=out_specs,
        out_shape=out_shape,
        scratch_shapes=[pltpu.VMEM((kdim, d), BF16)] if cast_w else [],
        compiler_params=_params("arbitrary", "arbitrary"),
        name="matmul_postnorm",
    )(*args)
    return (res[0], res[1]) if with_h else (res[0], None)


def _fox_gate_kernel(h_ref, wf_ref, bf_ref, o_ref, carry_ref, *, ts):
    @pl.when(pl.program_id(1) == 0)
    def _():
        carry_ref[...] = jnp.zeros_like(carry_ref)

    logit = _dot(h_ref[...], wf_ref[...]) + bf_ref[...]
    log_f = -_softplus(-logit)
    row = lax.broadcasted_iota(jnp.int32, (ts, ts), 0)
    col = lax.broadcasted_iota(jnp.int32, (ts, ts), 1)
    tri = jnp.where(row >= col, 1.0, 0.0).astype(BF16)
    hi, mid, lo = _split3(log_f)
    cs = _dot(tri, hi) + _dot(tri, mid) + _dot(tri, lo) + carry_ref[...]
    o_ref[...] = cs
    carry_ref[...] = cs[ts - 1:ts, :]


def _fox_gate(h, wf_pad, bf_pad, batch, seq, ts=256):
    m, d = h.shape
    spb = seq // ts
    return pl.pallas_call(
        functools.partial(_fox_gate_kernel, ts=ts),
        grid=(batch, spb),
        in_specs=[pl.BlockSpec((ts, d), lambda b, s: (b * spb + s, 0)),
                  pl.BlockSpec((d, LANES), lambda b, s: (0, 0)),
                  pl.BlockSpec((1, LANES), lambda b, s: (0, 0))],
        out_specs=pl.BlockSpec((ts, LANES), lambda b, s: (b * spb + s, 0)),
        out_shape=jax.ShapeDtypeStruct((m, LANES), F32),
        scratch_shapes=[pltpu.VMEM((1, LANES), F32)],
        compiler_params=_params("parallel", "arbitrary"),
        name="fox_gate",
    )(h, wf_pad, bf_pad)


def _fox_attn_kernel(q_ref, k_ref, v_ref, cq_ref, ck_ref, o_ref, *, tq, tk, scale):
    head = pl.program_id(1)
    i = pl.program_id(2)
    ratio = tq // tk
    hd = q_ref.shape[1]
    lane = lax.broadcasted_iota(jnp.int32, (tq, LANES), 1)
    cq = jnp.sum(jnp.where(lane == head, cq_ref[...], 0.0), axis=1, keepdims=True) * LOG2E
    row_id = lax.broadcasted_iota(jnp.int32, (tk, tk), 0)
    col_id = lax.broadcasted_iota(jnp.int32, (tk, tk), 1)

    def update(j, row0, masked, carry):
        start = pl.multiple_of(j * tk, tk)
        kj = k_ref[pl.ds(start, tk), :]
        vj = v_ref[pl.ds(start, tk), :]
        ckj = ck_ref[0, 0, j] * LOG2E
        m_prev, l_prev, acc = carry
        s = _dot_nt(q_ref[row0:, :], kj) * (scale * LOG2E) + cq[row0:] - ckj
        if masked:
            top = jnp.where(row_id >= col_id, s[:tk], -jnp.inf)
            s = top if tq - row0 == tk else jnp.concatenate([top, s[tk:]], axis=0)
        m_new = jnp.maximum(m_prev, jnp.max(s, axis=1, keepdims=True))
        alpha = jnp.exp2(m_prev - m_new)
        p = jnp.exp2(s - m_new)
        l_new = alpha * l_prev + jnp.sum(p, axis=1, keepdims=True)
        return m_new, l_new, alpha * acc + _dot(p.astype(BF16), vj)

    init = (jnp.full((tq, 1), -jnp.inf, F32), jnp.zeros((tq, 1), F32), jnp.zeros((tq, hd), F32))
    carry = lax.fori_loop(0, ratio * i, lambda j, c: update(j, 0, False, c), init)
    done = []
    for t in range(ratio):
        carry = update(ratio * i + t, t * tk, True, carry)
        done.append(tuple(x[:tk] for x in carry))
        if t + 1 < ratio:
            carry = tuple(x[tk:] for x in carry)
    l_fin = jnp.concatenate([d[1] for d in done], axis=0)
    acc = jnp.concatenate([d[2] for d in done], axis=0)
    o_ref[...] = (acc / l_fin).astype(o_ref.dtype)


def _fox_attention(qkv, cum, batch, seq, heads, tq=2048, tk=256):
    m = qkv.shape[0]
    hd = qkv.shape[1] // (3 * heads)
    nq = seq // tq
    nk = seq // tk
    ck = cum[:, :heads].reshape(batch, seq, heads).transpose(0, 2, 1).reshape(batch, heads, nk, 1, tk)
    return pl.pallas_call(
        functools.partial(_fox_attn_kernel, tq=tq, tk=tk, scale=hd ** -0.5),
        grid=(batch, heads, nq),
        in_specs=[pl.BlockSpec((tq, hd), lambda b, h, i: (b * nq + i, h)),
                  pl.BlockSpec((seq, hd), lambda b, h, i: (b, heads + h)),
                  pl.BlockSpec((seq, hd), lambda b, h, i: (b, 2 * heads + h)),
                  pl.BlockSpec((tq, LANES), lambda b, h, i: (b * nq + i, 0)),
                  pl.BlockSpec((1, 1, nk, 1, tk), lambda b, h, i: (b, h, 0, 0, 0))],
        out_specs=pl.BlockSpec((tq, hd), lambda b, h, i: (b * nq + i, h)),
        out_shape=jax.ShapeDtypeStruct((m, heads * hd), BF16),
        compiler_params=_params("parallel", "parallel", "arbitrary"),
        name="fox_attention",
    )(qkv, qkv, qkv, cum, ck)


def _ffn_in_kernel(h_ref, w1_ref, w2_ref, cw1_ref, cw2_ref, cb1_ref, cb2_ref, o_ref, tail_ref, wb_ref,
                   *, tm, sub, tpb):
    @pl.when(pl.program_id(1) % tpb == 0)
    def _():
        tail_ref[...] = jnp.zeros_like(tail_ref)

    @pl.when(pl.program_id(1) == 0)
    def _():
        wb_ref[0] = w1_ref[...].astype(BF16)
        wb_ref[1] = w2_ref[...].astype(BF16)

    row = lax.broadcasted_iota(jnp.int32, (SUBLANES, 1), 0)
    w_refs = (wb_ref.at[0], wb_ref.at[1])
    cw = (cw1_ref[...], cw2_ref[...])
    cb = (cb1_ref[...], cb2_ref[...])

    def up(r):
        rows = h_ref[r * sub:(r + 1) * sub, :]
        return [_dot(rows, w_ref[...]) for w_ref in w_refs]

    def shifted(u, tail, shift):
        rolled = pltpu.roll(u, shift, 0)
        head = rolled[:SUBLANES]
        for s in range(shift):
            src = SUBLANES - shift + s
            head = jnp.where(row == s, tail[src:src + 1], head)
        return jnp.concatenate([head, rolled[SUBLANES:]], axis=0)

    def conv(n, u, tail):
        return cw[n][0:1] * shifted(u, tail, 2) + cw[n][1:2] * shifted(u, tail, 1) + cw[n][2:3] * u + cb[n]

    nsub = tm // sub
    tails = [tail_ref[0], tail_ref[1]]
    u = up(0)
    for r in range(nsub):
        u_next = up(r + 1) if r + 1 < nsub else None
        x1 = conv(0, u[0], tails[0])
        x2 = conv(1, u[1], tails[1])
        gelu = 0.5 * x1 * (1.0 + lax.erf(x1 * (2.0 ** -0.5)))
        o_ref[r * sub:(r + 1) * sub, :] = (gelu * x2).astype(o_ref.dtype)
        tails = [x[sub - SUBLANES:] for x in u]
        u = u_next
    tail_ref[0] = tails[0]
    tail_ref[1] = tails[1]


def _ffn_in(h, w_in, conv_w, conv_b, layer, seq, tm=2048, tn=512, sub=256):
    m, d = h.shape
    f = w_in.shape[2] // 2
    nf = f // tn
    tpb = seq // tm
    return pl.pallas_call(
        functools.partial(_ffn_in_kernel, tm=tm, sub=sub, tpb=tpb),
        grid=(nf, m // tm),
        in_specs=[pl.BlockSpec((tm, d), lambda j, i: (i, 0)),
                  pl.BlockSpec((None, d, tn), lambda j, i: (layer, 0, j)),
                  pl.BlockSpec((None, d, tn), lambda j, i: (layer, 0, nf + j)),
                  pl.BlockSpec((None, CONV_WIDTH, tn), lambda j, i: (layer, 0, j)),
                  pl.BlockSpec((None, CONV_WIDTH, tn), lambda j, i: (layer, 0, nf + j)),
                  pl.BlockSpec((None, 1, tn), lambda j, i: (layer, 0, j)),
                  pl.BlockSpec((None, 1, tn), lambda j, i: (layer, 0, nf + j))],
        out_specs=pl.BlockSpec((tm, tn), lambda j, i: (i, j)),
        out_shape=jax.ShapeDtypeStruct((m, f), BF16),
        scratch_shapes=[pltpu.VMEM((2, SUBLANES, tn), F32), pltpu.VMEM((2, d, tn), BF16)],
        compiler_params=_params("parallel", "arbitrary"),
        name="ffn_in",
    )(h, w_in, w_in, conv_w, conv_w, conv_b, conv_b)


def _rwkv_pre_kernel(x_ref, halo_ref, g_ref, sc_ref, sh_ref, mu_ref, *o_refs, tm, tpb):
    first = pl.program_id(0) % tpb == 0
    inv = _inv_rms(x_ref[...])
    inv_halo = _inv_rms(halo_ref[...])
    row = lax.broadcasted_iota(jnp.int32, (SUBLANES, 1), 0)
    for c in range(x_ref.shape[1] // MXU_DIM):
        cols = slice(c * MXU_DIM, (c + 1) * MXU_DIM)
        mod = lambda t, s: t * s * g_ref[:, cols] * (1.0 + sc_ref[0, :, cols]) + sh_ref[0, :, cols]
        h = mod(x_ref[:, cols], inv)
        h_halo = jnp.where(first, 0.0, mod(halo_ref[:, cols], inv_halo))
        rolled = pltpu.roll(h, 1, 0)
        head = jnp.where(row == 0, h_halo[SUBLANES - 1:SUBLANES], rolled[:SUBLANES])
        xx = jnp.concatenate([head, rolled[SUBLANES:]], axis=0) - h
        for n, o_ref in enumerate(o_refs):
            o_ref[:, cols] = (h + xx * mu_ref[n:n + 1, cols]).astype(o_ref.dtype)


def _rwkv_pre(x2, g, sc, sh, mu, seq, tm=256):
    m, d = x2.shape
    tpb = seq // tm
    hb = tm // SUBLANES
    nmix = mu.shape[0]
    return pl.pallas_call(
        functools.partial(_rwkv_pre_kernel, tm=tm, tpb=tpb),
        grid=(m // tm,),
        in_specs=[pl.BlockSpec((tm, d), lambda i: (i, 0)),
                  pl.BlockSpec((SUBLANES, d), lambda i: (jnp.maximum(i * hb - 1, 0), 0)),
                  pl.BlockSpec((1, d), lambda i: (0, 0)),
                  pl.BlockSpec((1, 1, d), lambda i: (i // tpb, 0, 0)),
                  pl.BlockSpec((1, 1, d), lambda i: (i // tpb, 0, 0)),
                  pl.BlockSpec((nmix, d), lambda i: (0, 0))],
        out_specs=[pl.BlockSpec((tm, d), lambda i: (i, 0))] * nmix,
        out_shape=[jax.ShapeDtypeStruct((m, d), BF16)] * nmix,
        compiler_params=_params("parallel"),
        name="rwkv_pre",
    )(x2, x2, g.reshape(1, d), sc, sh, mu)


def _lora_down_kernel(x_ref, w1_ref, o_ref, *, act):
    t = _dot(x_ref[...], w1_ref[...])
    if act == "tanh":
        t = jnp.tanh(t)
    elif act == "sigmoid":
        t = jax.nn.sigmoid(t)
    o_ref[...] = t.astype(o_ref.dtype)


def _lora_down(x, w1, act, tm=1024):
    m, d = x.shape
    r = w1.shape[1]
    return pl.pallas_call(
        functools.partial(_lora_down_kernel, act=act),
        grid=(m // tm,),
        in_specs=[pl.BlockSpec((tm, d), lambda i: (i, 0)),
                  pl.BlockSpec((d, r), lambda i: (0, 0))],
        out_specs=pl.BlockSpec((tm, r), lambda i: (i, 0)),
        out_shape=jax.ShapeDtypeStruct((m, r), BF16),
        compiler_params=_params("parallel"),
        name="rwkv_lora_" + act,
    )(x, w1)


def _wkv_kernel(r_ref, k_ref, v_ref, tw_ref, ta_ref, tg_ref, w2_ref, a2_ref, g2_ref,
                w0_ref, a0_ref, kk_ref, ka_ref, rk_ref, gnw_ref, gnb_ref,
                o_ref, s_ref, *, groups, chunks):
    L = WKV_CHUNK
    N = RWKV_HEAD_DIM
    W = MXU_DIM

    @pl.when(pl.program_id(2) == 0)
    def _():
        s_ref[...] = jnp.zeros_like(s_ref)

    ri = lax.broadcasted_iota(jnp.int32, (W, W), 0)
    ci = lax.broadcasted_iota(jnp.int32, (W, W), 1)
    bd = (ri // N) == (ci // N)
    bd_bf = jnp.where(bd, 1.0, 0.0).astype(BF16)
    t_i = lax.broadcasted_iota(jnp.int32, (L, W), 0)
    s_i = lax.broadcasted_iota(jnp.int32, (L, W), 1) % N
    strict = t_i > s_i
    incl = t_i >= s_i
    eye = jnp.where(t_i == s_i, 1.0, 0.0)
    tri = jnp.where(lax.broadcasted_iota(jnp.int32, (L, L), 0) >= lax.broadcasted_iota(jnp.int32, (L, L), 1),
                    1.0, 0.0).astype(BF16)

    def block_diag(x):
        xb = x.astype(BF16)
        return jnp.concatenate([xb] * (W // L), axis=0) * bd_bf

    def stack(a, b):
        return jnp.concatenate([a, b], axis=0).astype(BF16)

    def head_sums(xs):
        res = _dot(jnp.concatenate(xs, axis=0).astype(BF16), bd_bf)
        return [res[n * L:(n + 1) * L] for n in range(len(xs))]

    def each(fn, *cols):
        return [fn(*args) for args in zip(*cols)]

    def cumsum(x):
        hi, mid, lo = _split3(x)
        return _dot(tri, hi) + _dot(tri, mid) + _dot(tri, lo)

    lanes = [slice(gi * W, (gi + 1) * W) for gi in range(groups)]

    def par(ref):
        return [ref[:, sl] for sl in lanes]

    def lora_up(t_ref, w_ref):
        return [_dot(t_ref[...], w_ref[:, sl]) for sl in lanes]

    wl_all, al_all, g_all = lora_up(tw_ref, w2_ref), lora_up(ta_ref, a2_ref), lora_up(tg_ref, g2_ref)

    def prepare(c):
        rows = slice(c * L, (c + 1) * L)
        tok = lambda ref: [ref[rows, sl] for sl in lanes]
        r, v, k_raw = tok(r_ref), tok(v_ref), tok(k_ref)
        wl, al, gate = ([x[rows] for x in xs] for xs in (wl_all, al_all, g_all))
        lw = each(lambda w0, wlx: -math.exp(-0.5) * jax.nn.sigmoid(w0 + wlx), par(w0_ref), wl)
        a = each(lambda a0, alx: jax.nn.sigmoid(a0 + alx), par(a0_ref), al)
        kk = each(lambda kx, kkp: kx * kkp, k_raw, par(kk_ref))
        k = each(lambda kx, ax, ka: kx * (1.0 + (ax - 1.0) * ka), k_raw, a, par(ka_ref))
        kk = each(lambda x, sq: x / jnp.maximum(jnp.sqrt(sq), 1e-12), kk, head_sums([x * x for x in kk]))
        cum = each(cumsum, lw)
        cum_last = [cx[L - 1:L] for cx in cum]
        p_inv = each(lambda cx: jnp.exp(-cx), cum)
        decay = [jnp.exp(cl) for cl in cum_last]
        p_rest = each(lambda pi, dx: pi * dx, p_inv, decay)
        b = each(lambda x, ax: x * ax, kk, a)
        lhs_ar = each(lambda x, cx, lx, rx: stack(-x * jnp.exp(cx - lx), rx * jnp.exp(cx)), kk, cum, lw, r)
        rkr = each(lambda rx, kx, rk: rx * kx * rk, r, k, par(rk_ref))
        bonus = each(lambda sx, vx: sx * vx, head_sums(rkr), v)
        kb_rest =each(lambda bx, kx, pr: stack(bx * pr, kx * pr), b, k, p_rest)
        b_w = each(lambda bx, pi: block_diag(bx * pi), b, p_inv)
        k_w = each(lambda kx, pi: block_diag(kx * pi), k, p_inv)
        ab = each(_dot_nt, lhs_ar, b_w)
        a_ab = [jnp.where(strict, x[:L], 0.0) for x in ab]
        a_rb = [jnp.where(incl, x[L:], 0.0) for x in ab]
        inv = [eye + x for x in a_ab]
        npow = [x.astype(BF16) for x in a_ab]
        npow = each(lambda n: _dot(n, block_diag(n)).astype(BF16), npow)
        for _ in range(4):
            res = each(lambda n, t: _dot(jnp.concatenate([n, t.astype(BF16)], axis=0), block_diag(n)), npow, inv)
            npow = [x[:L].astype(BF16) for x in res]
            inv = [t + x[L:] for t, x in zip(inv, res)]
        inv = each(lambda t, n: t + _dot(t.astype(BF16), block_diag(n)), inv, npow)
        ak = each(_dot_nt, lhs_ar, k_w)
        a_ak = [jnp.where(strict, x[:L], 0.0) for x in ak]
        a_rk = [jnp.where(incl, x[L:], 0.0) for x in ak]
        a_v = each(lambda x, y, vx: _dot(stack(x, y), block_diag(vx)), a_ak, a_rk, v)
        return dict(rows=rows, v=v, lhs_ar=lhs_ar, inv=inv, a_v=a_v, a_rb=a_rb, bonus=bonus,
                    decay=decay, kb_rest=kb_rest, gate=gate)

    prepared = [prepare(c) for c in range(chunks)]
    gn_w, gn_b = par(gnw_ref), par(gnb_ref)

    state = [s_ref[gi] for gi in range(groups)]
    for q in prepared:
        a_s = each(lambda lhs, s: _dot_nt(lhs, s.astype(BF16)), q["lhs_ar"], state)
        z = each(lambda t, av, as_: _dot(t.astype(BF16), block_diag(av[:L] + as_[:L])), q["inv"], q["a_v"], a_s)
        y = each(lambda as_, av, arb, zx: as_[L:] + av[L:] + _dot(arb.astype(BF16), block_diag(zx)),
                 a_s, q["a_v"], q["a_rb"], z)
        upd = each(lambda zx, vx, kb: _dot_tn(stack(zx, vx), kb), z, q["v"], q["kb_rest"])
        state = each(lambda s, dx, ux: jnp.where(bd, s * dx + ux, 0.0), state, q["decay"], upd)

        mean = [m * (1.0 / N) for m in head_sums(y)]
        yc = each(lambda yx, mx: yx - mx, y, mean)
        var = [m * (1.0 / N) for m in head_sums([x * x for x in yc])]
        for gi, sl in enumerate(lanes):
            yn = yc[gi] * lax.rsqrt(var[gi] + GN_EPS) * gn_w[gi] + gn_b[gi]
            o_ref[q["rows"], sl] = ((yn + q["bonus"][gi]) * q["gate"][gi]).astype(o_ref.dtype)
    for gi in range(groups):
        s_ref[gi] = state[gi]


def _wkv(r, k, v, lora, w0, a0, k_k, k_a, r_k, gn_w, gn_b, batch, seq, groups=8, chunks=4):
    m, d = r.shape
    rows = chunks * WKV_CHUNK
    gw = groups * MXU_DIM
    ng = d // gw
    nc = seq // rows
    tok = pl.BlockSpec((rows, gw), lambda b, gidx, c: (b * nc + c, gidx))
    par = pl.BlockSpec((1, gw), lambda b, gidx, c: (0, gidx))
    low = [pl.BlockSpec((rows, t.shape[1]), lambda b, gidx, c: (b * nc + c, 0)) for t, _ in lora]
    up = [pl.BlockSpec((w2.shape[0], gw), lambda b, gidx, c: (0, gidx)) for _, w2 in lora]
    row = lambda t: t.reshape(1, d)
    return pl.pallas_call(
        functools.partial(_wkv_kernel, groups=groups, chunks=chunks),
        grid=(batch, ng, nc),
        in_specs=[tok] * 3 + low + up + [par] * 7,
        out_specs=tok,
        out_shape=jax.ShapeDtypeStruct((m, d), BF16),
        scratch_shapes=[pltpu.VMEM((groups, MXU_DIM, MXU_DIM), F32)],
        compiler_params=_params("parallel", "parallel", "arbitrary"),
        name="wkv7",
    )(r, k, v, *[t for t, _ in lora], *[w2 for _, w2 in lora],
      row(w0), row(a0), row(k_k), row(k_a), row(r_k), row(gn_w), row(gn_b))


def _pad_to(x, axis, size):
    pad = [(0, 0)] * x.ndim
    pad[axis] = (0, size - x.shape[axis])
    return jnp.pad(x, pad)


def _round_up(n, mult):
    return -(-n // mult) * mult


def kernel(x, c, mod_w, mod_b, norm_g, fox_w_in, fox_b_f, fox_w_out, rwkv_mu, rwkv_w0, rwkv_w1, rwkv_w2, rwkv_a0, rwkv_a1, rwkv_a2, rwkv_g1, rwkv_g2, rwkv_k_k, rwkv_k_a, rwkv_r_k, rwkv_w_r, rwkv_w_k, rwkv_w_v, rwkv_w_o, rwkv_gn_w, rwkv_gn_b, ffn_w_in, ffn_conv_w, ffn_conv_b, ffn_w_out):
    batch, seq, d = x.shape
    depth = mod_w.shape[0]
    heads = fox_b_f.shape[1]
    bf = lambda t: t.astype(BF16)

    mod = _modulation(c, mod_w, mod_b)
    x2 = x.reshape(batch * seq, d)
    ffn_w_out_bf = bf(ffn_w_out)

    h = None
    for i in range(depth):
        sh_m, sc_m, g_m, sh_f, sc_f, g_f = [mod[i, :, n] for n in range(6)]
        j = i // 2
        ffn_pre = (norm_g[i, 2], sc_f, sh_f)
        if i % 2 == 0:
            if h is None:
                h = _prenorm(x2, norm_g[i, 0], sc_m, sh_m, seq)
            qkv = _matmul(h, jnp.swapaxes(fox_w_in, 1, 2), j, 3 * d, BF16, w_is_transposed=True, name="fox_qkv")
            wf = _pad_to(bf(fox_w_in[j, :, 3 * d:]), 1, LANES)
            bfp = _pad_to(fox_b_f[j].reshape(1, heads), 1, LANES)
            cum = _fox_gate(h, wf, bfp, batch, seq)
            o = _fox_attention(qkv, cum, batch, seq, heads)
            x2, h = _matmul_postnorm(o, fox_w_out, j, x2, norm_g[i, 1], g_m, seq, nxt=ffn_pre, sub=512)
        else:
            xr, xw, xk, xv, xa, xg = _rwkv_pre(x2, norm_g[i, 0], sc_m, sh_m, rwkv_mu[j], seq)
            r = _matmul(xr, rwkv_w_r, j, d, F32, name="rwkv_r")
            k = _matmul(xk, rwkv_w_k, j, d, F32, name="rwkv_k")
            v = _matmul(xv, rwkv_w_v, j, d, F32, name="rwkv_v")
            def low_rank(xin, w1, w2, act):
                rank = _round_up(w1.shape[1], LANES)
                return _lora_down(xin, _pad_to(bf(w1), 1, rank), act), _pad_to(bf(w2), 0, rank)

            lora = [low_rank(xw, rwkv_w1[j], rwkv_w2[j], "tanh"),
                    low_rank(xa, rwkv_a1[j], rwkv_a2[j], "none"),
                    low_rank(xg, rwkv_g1[j], rwkv_g2[j], "sigmoid")]
            o = _wkv(r, k, v, lora, rwkv_w0[j], rwkv_a0[j], rwkv_k_k[j], rwkv_k_a[j],
                     rwkv_r_k[j], rwkv_gn_w[j], rwkv_gn_b[j], batch, seq)
            x2, h = _matmul_postnorm(o, rwkv_w_o, j, x2, norm_g[i, 1], g_m, seq, nxt=ffn_pre, sub=512)
        act = _ffn_in(h, ffn_w_in, ffn_conv_w, ffn_conv_b.reshape(depth, 1, -1), i, seq)
        f = act.shape[1]
        nxt = None
        if i + 1 < depth and (i + 1) % 2 == 0:
            nxt = (norm_g[i + 1, 0], mod[i + 1, :, 1], mod[i + 1, :, 0])
        x2, h = _matmul_postnorm(act, ffn_w_out_bf, i, x2, norm_g[i, 3], g_f, seq, nxt=nxt, tk=f // 2, sub=512)
    return x2.reshape(batch, seq, d)
```
